```python
import jax
import jax.numpy as jnp
from jax import lax
import numpy as np

D_MODEL = 2048
BATCH = 16
SEQ = 256
DEPTH = 2
DEC_BATCH = 4
DEC_SEQ = 4096
PAST_LEN = 512

GRID_W = 64
BRANCH_W = D_MODEL // 2
D_A = BRANCH_W
CHUNK_A = 128
CH_A = 128
G_A = D_A // CH_A
DH_B = 128
H_B = BRANCH_W // DH_B
MAX_KH = 8
KW = 16
ROPE_BASE = 10000.0
Q_BLOCK = 128
DK = 128
DV = 128
H_C = BRANCH_W // DV
CHUNK_C = 32
N_GROUPS = 4
EXPERTS_PER_GROUP = 8
N_EXPERTS = N_GROUPS * EXPERTS_PER_GROUP
TOP_K = 2
D_EXPERT = D_MODEL // 2
MOE_BLOCK = 256
LN_EPS = 1e-5
MASK_VALUE = -1e30
F_MIN = 1e-6
IN_SIZES = (2 * D_A, 3 * H_B * DH_B, H_C * DK, H_C * DK, H_C * DK, H_C * DV, H_C * DV, 3 * D_MODEL)
IN_COLS = 2 * D_A + 3 * H_B * DH_B + 3 * H_C * DK + 2 * H_C * DV + 3 * D_MODEL

kernel_name = 'hybrid_diffusion_gated_trunk_step'


def layer_norm(x, w, b):
    xf = x.astype(jnp.float32)
    mu = jnp.mean(xf, axis=-1, keepdims=True)
    var = jnp.mean(jnp.square(xf - mu), axis=-1, keepdims=True)
    y = (xf - mu) * lax.rsqrt(var + LN_EPS)
    return (y * w.astype(jnp.float32) + b.astype(jnp.float32)).astype(x.dtype)


def plain_norm(x):
    xf = x.astype(jnp.float32)
    mu = jnp.mean(xf, axis=-1, keepdims=True)
    var = jnp.mean(jnp.square(xf - mu), axis=-1, keepdims=True)
    return ((xf - mu) * lax.rsqrt(var + LN_EPS)).astype(x.dtype)


def modulation(cond, w_mod, b_mod):
    m = jax.nn.silu(cond) @ w_mod + b_mod
    return jnp.split(m[..., None, :], 6, axis=-1)


def in_projection(h, w_in):
    idx, acc = [], 0
    for s in IN_SIZES[:-1]:
        acc += s
        idx.append(acc)
    return [h @ w for w in jnp.split(w_in, idx, axis=-1)]


def gmlp_mixer(uv, ws, bs):
    u, v = jnp.split(jax.nn.gelu(uv), 2, axis=-1)
    v = plain_norm(v)
    B, L, _ = v.shape
    vg = v.reshape(B, L // CHUNK_A, CHUNK_A, G_A, CH_A)
    sv = jnp.einsum('gts,bnsgc->bntgc', ws, vg) + bs.T[None, None, :, :, None]
    return u * sv.reshape(B, L, D_A)


def context_attention(q, k, v):
    B, S, H, dh = q.shape
    scale = dh ** -0.5
    qb = q.reshape(B, S // Q_BLOCK, Q_BLOCK, H, dh).transpose(1, 0, 2, 3, 4)

    def block(qblk):
        s = jnp.einsum('bqhd,bkhd->bhqk', qblk, k) * scale
        p = jax.nn.softmax(s.astype(jnp.float32), axis=-1).astype(v.dtype)
        return jnp.einsum('bhqk,bkhd->bqhd', p, v)

    o = lax.map(block, qb)
    return o.transpose(1, 0, 2, 3, 4).reshape(B, S, H * dh)


def axial_rope(n_tokens, dtype):
    pos = jnp.arange(n_tokens)
    row = (pos // GRID_W).astype(jnp.float32)
    col = (pos % GRID_W).astype(jnp.float32)
    n_freq = DH_B // 4
    inv = ROPE_BASE ** (-jnp.arange(n_freq, dtype=jnp.float32) / n_freq)
    ang = jnp.concatenate([row[:, None] * inv, col[:, None] * inv], axis=-1)
    return jnp.cos(ang)[:, None, :].astype(dtype), jnp.sin(ang)[:, None, :].astype(dtype)


def apply_rope(x, cos, sin):
    half = x.shape[-1] // 2
    x1, x2 = x[..., :half], x[..., half:]
    return jnp.concatenate([x1 * cos - x2 * sin, x1 * sin + x2 * cos], axis=-1)


def neighbourhood_attention(q, k, v, k_ctx, v_ctx, rpb):
    B, L, H, dh = q.shape
    rows = L // GRID_W
    kh = min(MAX_KH, rows)
    scale = dh ** -0.5
    cos, sin = axial_rope(L, q.dtype)
    q_rot = apply_rope(q, cos, sin).reshape(B, rows, GRID_W, H, dh)
    k_rot = apply_rope(k, cos, sin).reshape(B, rows, GRID_W, H, dh)
    q_raw = q.reshape(B, rows, GRID_W, H, dh)
    v_g = v.reshape(B, rows, GRID_W, H, dh)
    cols = jnp.arange(GRID_W)
    col_start = jnp.clip(cols - KW // 2, 0, GRID_W - KW)
    col_mask = (cols[None, :] >= col_start[:, None]) & (cols[None, :] < col_start[:, None] + KW)
    dx_idx = jnp.clip(cols[None, :] - cols[:, None] + KW - 1, 0, 2 * KW - 2)

    def row_step(r):
        start = jnp.clip(r - kh // 2, 0, rows - kh)
        q_row = lax.dynamic_index_in_dim(q_rot, r, axis=1, keepdims=False)
        qn_row = lax.dynamic_index_in_dim(q_raw, r, axis=1, keepdims=False)
        k_blk = lax.dynamic_slice_in_dim(k_rot, start, kh, axis=1)
        v_blk = lax.dynamic_slice_in_dim(v_g, start, kh, axis=1)
        dy_idx = start + jnp.arange(kh) - r + MAX_KH - 1
        bias = rpb[:, dy_idx[:, None, None], dx_idx[None, :, :]].transpose(0, 2, 1, 3)
        s_loc = (jnp.einsum('bqhd,bikhd->bhqik', q_row, k_blk) * scale + bias[None]).astype(jnp.float32)
        s_loc = jnp.where(col_mask[:, None, :], s_loc, MASK_VALUE)
        s_ctx = (jnp.einsum('bqhd,bphd->bhqp', qn_row, k_ctx) * scale).astype(jnp.float32)
        s = jnp.concatenate([s_loc.reshape(B, H, GRID_W, kh * GRID_W), s_ctx], axis=-1)
        p = jax.nn.softmax(s, axis=-1).astype(v.dtype)
        p_loc = p[..., :kh * GRID_W].reshape(B, H, GRID_W, kh, GRID_W)
        p_ctx = p[..., kh * GRID_W:]
        return (jnp.einsum('bhqik,bikhd->bqhd', p_loc, v_blk)
                + jnp.einsum('bhqp,bphd->bqhd', p_ctx, v_ctx))

    o = lax.map(row_step, jnp.arange(rows))
    return o.transpose(1, 0, 2, 3, 4).reshape(B, L, H * dh)


def forget_log(z, lb):
    f = lb + (1.0 - lb) * jax.nn.sigmoid(z)
    return jnp.log(jnp.maximum(f, F_MIN))


def chunk_scan(q, k, v, log_f, s0):
    B, L, H, dk = q.shape
    dv = v.shape[-1]
    nc = L // CHUNK_C

    def to_chunks(t):
        return t.reshape(B, nc, CHUNK_C, H, t.shape[-1]).transpose(1, 0, 3, 2, 4)

    tril = jnp.tril(jnp.ones((CHUNK_C, CHUNK_C), dtype=bool))[None, None, :, :, None]

    def step(S, inp):
        qc, kc, vc, gc = inp
        A = jnp.cumsum(gc, axis=2)
        o_inter = jnp.einsum('bhtk,bhkv->bhtv', qc * jnp.exp(A), S)
        diff = A[:, :, :, None, :] - A[:, :, None, :, :]
        decay = jnp.exp(jnp.where(tril, diff, MASK_VALUE))
        scores = jnp.einsum('bhtk,bhtsk,bhsk->bhts', qc, decay, kc)
        o_intra = jnp.einsum('bhts,bhsv->bhtv', scores, vc)
        a_last = A[:, :, -1:, :]
        S_new = (jnp.exp(a_last[:, :, 0, :])[..., None] * S
                 + jnp.einsum('bhsk,bhsv->bhkv', kc * jnp.exp(a_last - A), vc))
        return S_new, o_inter + o_intra

    s_fin, o = lax.scan(step, s0, (to_chunks(q), to_chunks(k), to_chunks(v), to_chunks(log_f)))
    return s_fin, o.transpose(1, 0, 3, 2, 4).reshape(B, L, H, dv)


def hgrn2_mixer(hq, hf_f, hf_b, hi, hg, lower, norm_w, s0_f, s0_b):
    B, L, _ = hq.shape
    f32 = jnp.float32
    q = hq.reshape(B, L, H_C, DK).astype(f32)
    v = hi.reshape(B, L, H_C, DV).astype(f32)
    logf_f = forget_log(hf_f.reshape(B, L, H_C, DK).astype(f32), lower[0])
    logf_b = forget_log(hf_b.reshape(B, L, H_C, DK).astype(f32), lower[1])
    s_f, o_f = chunk_scan(q, -jnp.expm1(logf_f), v, logf_f, s0_f.astype(f32))
    rev = lambda t: jnp.flip(t, axis=1)
    s_b, o_b = chunk_scan(rev(q), rev(-jnp.expm1(logf_b)), rev(v), rev(logf_b), s0_b.astype(f32))
    o = o_f + rev(o_b)
    o = o * lax.rsqrt(jnp.mean(jnp.square(o), axis=-1, keepdims=True) + LN_EPS) * norm_w.astype(f32)
    o = o.astype(hq.dtype) * jax.nn.silu(hg).reshape(B, L, H_C, DV)
    return o.reshape(B, L, H_C * DV), s_f, s_b


def merge_branches(gates, y_a, y_b, y_c, w_branch, w_o):
    g_a, g_b, g_c = jnp.split(jax.nn.sigmoid(gates), 3, axis=-1)
    merged = g_a * (y_a @ w_branch[0]) + g_b * (y_b @ w_branch[1]) + g_c * (y_c @ w_branch[2])
    return merged @ w_o


def routed_experts(x, expert_ids, weights, w1, w3, w2):
    T, D = x.shape
    K = expert_ids.shape[1]
    flat_e = expert_ids.reshape(-1)
    flat_w = weights.reshape(-1)
    flat_tok = jnp.arange(T * K, dtype=jnp.int32) // K
    order = jnp.argsort(flat_e)
    se, stok, sw = flat_e[order], flat_tok[order], flat_w[order]
    counts = jax.ops.segment_sum(jnp.ones_like(flat_e), flat_e, num_segments=N_EXPERTS)
    padded = (counts + MOE_BLOCK - 1) // MOE_BLOCK * MOE_BLOCK
    seg_start = jnp.cumsum(counts) - counts
    pad_end = jnp.cumsum(padded)
    pad_start = pad_end - padded
    dest = pad_start[se] + (jnp.arange(T * K, dtype=jnp.int32) - seg_start[se])
    n_blocks = -(-(T * K) // MOE_BLOCK) + N_EXPERTS
    n_rows = n_blocks * MOE_BLOCK
    row_tok = jnp.full((n_rows,), T, dtype=jnp.int32).at[dest].set(stok)
    row_w = jnp.zeros((n_rows,), dtype=x.dtype).at[dest].set(sw)
    block_expert = jnp.clip(jnp.searchsorted(pad_end, jnp.arange(n_blocks) * MOE_BLOCK, side='right'),
                            0, N_EXPERTS - 1)
    x_pad = jnp.concatenate([x, jnp.zeros((1, D), x.dtype)], axis=0)

    def block_step(args):
        tok, e = args
        xb = x_pad[tok]
        hb = jax.nn.silu(xb @ w1[e]) * (xb @ w3[e])
        return hb @ w2[e]

    yb = lax.map(block_step, (row_tok.reshape(n_blocks, MOE_BLOCK), block_expert))
    y = jnp.zeros((T + 1, D), x.dtype).at[row_tok].add(yb.reshape(n_rows, D) * row_w[:, None])
    return y[:T]


def hier_moe(h, lp):
    B, L, D = h.shape
    T = B * L
    x = h.reshape(T, D)
    coarse = (x @ lp['router_w1'] + lp['router_b1']).astype(jnp.float32)
    g_val, g_idx = lax.top_k(coarse, 1)
    p_group = jnp.exp(g_val - jax.nn.logsumexp(coarse, axis=-1, keepdims=True))
    fine = (jnp.einsum('td,gde->tge', x, lp['router_w2']) + lp['router_b2']).astype(jnp.float32)
    sel = jnp.broadcast_to(g_idx[:, :, None], (T, 1, EXPERTS_PER_GROUP))
    fine_sel = jnp.take_along_axis(fine, sel, axis=1)[:, 0]
    e_val, e_idx = lax.top_k(fine_sel, TOP_K)
    weights = jax.nn.softmax(e_val, axis=-1) * p_group
    expert_ids = g_idx * EXPERTS_PER_GROUP + e_idx
    y = routed_experts(x, expert_ids, weights.astype(x.dtype), lp['moe_w1'], lp['moe_w3'], lp['moe_w2'])
    return y.reshape(B, L, D)


def context_layer(x, c_ctx, lp, alpha):
    sh1, sc1, g1, sh2, sc2, g2 = modulation(c_ctx, lp['w_mod'], lp['b_mod'])
    h = x * (1 + sc1) + sh1
    uv, qkv, hq, hf_f, hf_b, hi, hg, gates = in_projection(h, lp['w_in'])
    B, S, _ = x.shape
    y_a = gmlp_mixer(uv, lp['gmlp_ws'], lp['gmlp_bs'])
    q, k, v = [t.reshape(B, S, H_B, DH_B) for t in jnp.split(qkv, 3, axis=-1)]
    y_b = context_attention(q, k, v)
    s0 = jnp.zeros((B, H_C, DK, DV), jnp.float32)
    y_c, s_f, s_b = hgrn2_mixer(hq, hf_f, hf_b, hi, hg, lp['lower'], lp['hgrn_norm_w'], s0, s0)
    mix = merge_branches(gates, y_a, y_b, y_c, lp['w_branch'], lp['w_o'])
    x = layer_norm(alpha * x + g1 * mix, lp['ln1_w'], lp['ln1_b'])
    h2 = x * (1 + sc2) + sh2
    x = layer_norm(alpha * x + g2 * hier_moe(h2, lp), lp['ln2_w'], lp['ln2_b'])
    return x, k, v, jnp.stack([s_f, s_b], axis=1)


def latent_layer(x, c, k_ctx, v_ctx, state_ctx, lp, alpha):
    sh1, sc1, g1, sh2, sc2, g2 = modulation(c, lp['w_mod'], lp['b_mod'])
    h = x * (1 + sc1) + sh1
    uv, qkv, hq, hf_f, hf_b, hi, hg, gates = in_projection(h, lp['w_in'])
    B, L, _ = x.shape
    y_a = gmlp_mixer(uv, lp['gmlp_ws'], lp['gmlp_bs'])
    q, k, v = [t.reshape(B, L, H_B, DH_B) for t in jnp.split(qkv, 3, axis=-1)]
    y_b = neighbourhood_attention(q, k, v, k_ctx, v_ctx, lp['na_rpb'])
    y_c, _, _ = hgrn2_mixer(hq, hf_f, hf_b, hi, hg, lp['lower'], lp['hgrn_norm_w'],
                            state_ctx[:, 0], state_ctx[:, 1])
    mix = merge_branches(gates, y_a, y_b, y_c, lp['w_branch'], lp['w_o'])
    x = layer_norm(alpha * x + g1 * mix, lp['ln1_w'], lp['ln1_b'])
    h2 = x * (1 + sc2) + sh2
    return layer_norm(alpha * x + g2 * hier_moe(h2, lp), lp['ln2_w'], lp['ln2_b'])


def setup_inputs(seed: int = 0) -> dict:
    key = jax.random.key(seed)
    ks = jax.random.split(key, 32)
    f32 = jnp.float32
    D = D_MODEL
    beta = (8.0 * DEPTH) ** -0.25

    def nrm(k, shape, s):
        return s * jax.random.normal(k, shape, f32)

    return {
        'x_prompt': nrm(ks[0], (BATCH, SEQ, D), 1.0),
        'x_sample': nrm(ks[1], (DEC_BATCH, DEC_SEQ, D), 1.0),
        'cache_k': nrm(ks[2], (DEC_BATCH, DEPTH, PAST_LEN, H_B, DH_B), 1.0),
        'cache_v': nrm(ks[3], (DEC_BATCH, DEPTH, PAST_LEN, H_B, DH_B), 1.0),
        'state_hgrn': nrm(ks[4], (DEC_BATCH, DEPTH, 2, H_C, DK, DV), 0.5),
        'c': nrm(ks[5], (DEC_BATCH, D), 1.0),
        'c_ctx': nrm(ks[6], (D,), 1.0),
        'w_mod': nrm(ks[7], (DEPTH, D, 6 * D), 0.5 * D ** -0.5),
        'b_mod': nrm(ks[8], (DEPTH, 6 * D), 0.02),
        'w_in': nrm(ks[9], (DEPTH, D, IN_COLS), D ** -0.5),
        'gmlp_ws': nrm(ks[10], (DEPTH, G_A, CHUNK_A, CHUNK_A), CHUNK_A ** -0.5),
        'gmlp_bs': 1.0 + nrm(ks[11], (DEPTH, G_A, CHUNK_A), 0.02),
        'na_rpb': nrm(ks[12], (DEPTH, H_B, 2 * MAX_KH - 1, 2 * KW - 1), 0.5),
        'hgrn_lb': nrm(ks[13], (DEPTH, 2, H_C, DK), 1.0),
        'hgrn_norm_w': 1.0 + nrm(ks[14], (DEPTH, DV), 0.02),
        'w_branch': nrm(ks[15], (DEPTH, 3, BRANCH_W, D), beta * BRANCH_W ** -0.5),
        'w_o': nrm(ks[16], (DEPTH, D, D), beta * D ** -0.5),
        'ln1_w': 1.0 + nrm(ks[17], (DEPTH, D), 0.02),
        'ln1_b': nrm(ks[18], (DEPTH, D), 0.02),
        'ln2_w': 1.0 + nrm(ks[19], (DEPTH, D), 0.02),
        'ln2_b': nrm(ks[20], (DEPTH, D), 0.02),
        'router_w1': nrm(ks[21], (DEPTH, D, N_GROUPS), D ** -0.5),
        'router_b1': nrm(ks[22], (DEPTH, N_GROUPS), 0.01),
        'router_w2': nrm(ks[23], (DEPTH, N_GROUPS, D, EXPERTS_PER_GROUP), D ** -0.5),
        'router_b2': nrm(ks[24], (DEPTH, N_GROUPS, EXPERTS_PER_GROUP), 0.01),
        'moe_w1': nrm(ks[25], (DEPTH, N_EXPERTS, D, D_EXPERT), D ** -0.5),
        'moe_w3': nrm(ks[26], (DEPTH, N_EXPERTS, D, D_EXPERT), D ** -0.5),
        'moe_w2': nrm(ks[27], (DEPTH, N_EXPERTS, D_EXPERT, D), beta * D_EXPERT ** -0.5),
    }


def reference(x_prompt, x_sample, cache_k, cache_v, state_hgrn, c, c_ctx, w_mod, b_mod, w_in,
              gmlp_ws, gmlp_bs, na_rpb, hgrn_lb, hgrn_norm_w, w_branch, w_o, ln1_w, ln1_b,
              ln2_w, ln2_b, router_w1, router_b1, router_w2, router_b2, moe_w1, moe_w3, moe_w2):
    alpha = (2.0 * DEPTH) ** 0.25
    p_lb = jax.nn.softmax(hgrn_lb.astype(jnp.float32), axis=0)
    lower_all = jnp.cumsum(p_lb, axis=0) - p_lb[0:1]
    xp, xs = x_prompt, x_sample
    ks_out, vs_out, ss_out = [], [], []
    for l in range(DEPTH):
        lp = {
            'w_mod': w_mod[l], 'b_mod': b_mod[l], 'w_in': w_in[l],
            'gmlp_ws': gmlp_ws[l], 'gmlp_bs': gmlp_bs[l], 'na_rpb': na_rpb[l],
            'lower': lower_all[l], 'hgrn_norm_w': hgrn_norm_w[l],
            'w_branch': w_branch[l], 'w_o': w_o[l],
            'ln1_w': ln1_w[l], 'ln1_b': ln1_b[l], 'ln2_w': ln2_w[l], 'ln2_b': ln2_b[l],
            'router_w1': router_w1[l], 'router_b1': router_b1[l],
            'router_w2': router_w2[l], 'router_b2': router_b2[l],
            'moe_w1': moe_w1[l], 'moe_w3': moe_w3[l], 'moe_w2': moe_w2[l],
        }
        xp, k_l, v_l, s_l = context_layer(xp, c_ctx, lp, alpha)
        ks_out.append(k_l)
        vs_out.append(v_l)
        ss_out.append(s_l)
        xs = latent_layer(xs, c, cache_k[:, l], cache_v[:, l], state_hgrn[:, l], lp, alpha)
    new_cache_k = jnp.stack(ks_out, axis=1)
    new_cache_v = jnp.stack(vs_out, axis=1)
    new_state_hgrn = jnp.stack(ss_out, axis=1)
    return (xp, xs, new_cache_k, new_cache_v, new_state_hgrn)
```

```python
import functools

import jax
import jax.numpy as jnp
from jax import lax
from jax.experimental import pallas as pl
from jax.experimental.pallas import tpu as pltpu

F32 = jnp.float32
BF16 = jnp.bfloat16

D_MODEL = 2048
BATCH = 16
SEQ = 256
DEPTH = 2
DEC_BATCH = 4
DEC_SEQ = 4096
PAST_LEN = 512
GRID_W = 64
BRANCH_W = D_MODEL // 2
CHUNK_A = 128
G_A = 8
DH_B = 128
H_B = 8
MAX_KH = 8
KW = 16
ROPE_BASE = 10000.0
DK = 128
H_C = 8
N_GROUPS = 4
EXPERTS_PER_GROUP = 8
N_EXPERTS = 32
TOP_K = 2
D_EXPERT = D_MODEL // 2
LN_EPS = 1e-5
MASK_VALUE = -1e30
F_MIN = 1e-6
IN_COLS = 16384

T_CTX = BATCH * SEQ
T_LAT = DEC_BATCH * DEC_SEQ
T_ALL = T_CTX + T_LAT
SLAB = 4096
N_SLAB = T_ALL // SLAB
MOD_ROWS = 8
LANE = 128
CB_Q, CB_K, CB_V = 16, 24, 32
CB_HQ, CB_HFF, CB_HFB, CB_HI, CB_HG = 40, 48, 56, 64, 72
CB_GATES = 80

HG_CHUNK = 128
HG_LEVELS = (8, 16, 32, 64, 128)
MOE_BLK = 256
N_MOE_ROWS = T_ALL * TOP_K + N_EXPERTS * MOE_BLK
N_MOE_BLOCKS = N_MOE_ROWS // MOE_BLK
ROUTE_LANES = 128

VMEM_LIMIT = 56 * 1024 * 1024


def _params(sem):
    return pltpu.CompilerParams(dimension_semantics=sem, vmem_limit_bytes=VMEM_LIMIT)


def _sigmoid(x):
    return 1.0 / (1.0 + jnp.exp(-x))


def _silu(x):
    return x * _sigmoid(x)


def _gelu_tanh(x):
    return x * (0.5 * (1.0 + jnp.tanh(0.7978845608028654 * (x + 0.044715 * (x * x * x)))))


def _layer_norm_rows(y, w, b):
    mu = jnp.mean(y, axis=-1, keepdims=True)
    yc = y - mu
    var = jnp.mean(yc * yc, axis=-1, keepdims=True)
    return yc * lax.rsqrt(var + LN_EPS) * w + b


def _dot(a, b):
    return jnp.dot(a, b, preferred_element_type=F32)


def _dot_nt(a, b):
    return lax.dot_general(a, b, (((1,), (1,)), ((), ())), preferred_element_type=F32)


def _dot_tn(a, b):
    return lax.dot_general(a, b, (((0,), (0,)), ((), ())), preferred_element_type=F32)


def _mod_kernel(c_ref, w_ref, b_ref, o_ref):
    o_ref[0] = _dot(_silu(c_ref[...]), w_ref[0]) + b_ref[0]


def _modulation(cond, w_mod, b_mod):
    tn = 1024
    n = 6 * D_MODEL
    out = pl.pallas_call(
        _mod_kernel,
        grid=(DEPTH, n // tn),
        in_specs=[
            pl.BlockSpec((MOD_ROWS, D_MODEL), lambda l, j: (0, 0)),
            pl.BlockSpec((1, D_MODEL, tn), lambda l, j: (l, 0, j)),
            pl.BlockSpec((1, 1, tn), lambda l, j: (l, 0, j)),
        ],
        out_specs=pl.BlockSpec((1, MOD_ROWS, tn), lambda l, j: (l, 0, j)),
        out_shape=jax.ShapeDtypeStruct((DEPTH, MOD_ROWS, n), F32),
        compiler_params=_params(("arbitrary", "arbitrary")),
        name="modulation",
    )(cond, w_mod, b_mod.reshape(DEPTH, 1, n))
    return out.reshape(DEPTH, MOD_ROWS, 6, D_MODEL)


def _inproj_kernel(x_ref, mod_ref, w_ref, o_ref, wbf_ref):
    @pl.when(pl.program_id(1) == 0)
    def _():
        wbf_ref[...] = w_ref[0].astype(BF16)

    h = x_ref[...] * (1.0 + mod_ref[0, 0, 1:2, :]) + mod_ref[0, 0, 0:1, :]
    o_ref[...] = _dot(h.astype(BF16), wbf_ref[...])


def _in_projection(x, mods, w_in, layer):
    tm, tn = 512, 1024
    return pl.pallas_call(
        _inproj_kernel,
        grid=(IN_COLS // tn, T_ALL // tm),
        in_specs=[
            pl.BlockSpec((tm, D_MODEL), lambda j, i: (i, 0)),
            pl.BlockSpec((1, 1, 6, D_MODEL), lambda j, i: (layer, (i * tm) // SLAB, 0, 0)),
            pl.BlockSpec((1, D_MODEL, tn), lambda j, i: (layer, 0, j)),
        ],
        out_specs=pl.BlockSpec((tm, tn), lambda j, i: (i, j)),
        out_shape=jax.ShapeDtypeStruct((T_ALL, IN_COLS), F32),
        scratch_shapes=[pltpu.VMEM((D_MODEL, tn), BF16)],
        compiler_params=_params(("arbitrary", "arbitrary")),
        name="in_projection",
    )(x, mods, w_in)


def _gmlp_kernel(uv_ref, ws_ref, bs_ref, o_ref, *, tm):
    ge = _gelu_tanh(uv_ref[...])
    u = ge[:, :BRANCH_W]
    v = ge[:, BRANCH_W:]
    mu = jnp.mean(v, axis=-1, keepdims=True)
    vc = v - mu
    var = jnp.mean(vc * vc, axis=-1, keepdims=True)
    vn = (vc * lax.rsqrt(var + LN_EPS)).astype(BF16)
    for n in range(tm // CHUNK_A):
        r = slice(n * CHUNK_A, (n + 1) * CHUNK_A)
        for g in range(G_A):
            c = slice(g * LANE, (g + 1) * LANE)
            sv = _dot(ws_ref[0, g], vn[r, c]) + bs_ref[0, :, c]
            o_ref[r, c] = u[r, c] * sv


def _gmlp_mixer(proj, ws_bf, bs_b, layer):
    tm = 256
    return pl.pallas_call(
        functools.partial(_gmlp_kernel, tm=tm),
        grid=(T_ALL // tm,),
        in_specs=[
            pl.BlockSpec((tm, 2 * BRANCH_W), lambda i: (i, 0)),
            pl.BlockSpec((1, G_A, CHUNK_A, CHUNK_A), lambda i: (layer, 0, 0, 0)),
            pl.BlockSpec((1, CHUNK_A, BRANCH_W), lambda i: (layer, 0, 0)),
        ],
        out_specs=pl.BlockSpec((tm, BRANCH_W), lambda i: (i, 0)),
        out_shape=jax.ShapeDtypeStruct((T_ALL, BRANCH_W), F32),
        compiler_params=_params(("arbitrary",)),
        name="gmlp_mixer",
    )(proj, ws_bf, bs_b)


def _ctx_attn_kernel(q_ref, k_ref, v_ref, o_ref):
    scale = DH_B ** -0.5
    s = _dot_nt(q_ref[...].astype(BF16), k_ref[...].astype(BF16)) * scale
    m = jnp.max(s, axis=-1, keepdims=True)
    p = jnp.exp(s - m)
    den = jnp.sum(p, axis=-1, keepdims=True)
    o_ref[...] = _dot(p.astype(BF16), v_ref[...].astype(BF16)) / den


def _context_attention(proj):
    return pl.pallas_call(
        _ctx_attn_kernel,
        grid=(BATCH, H_B),
        in_specs=[
            pl.BlockSpec((SEQ, DH_B), lambda b, h: (b, CB_Q + h)),
            pl.BlockSpec((SEQ, DH_B), lambda b, h: (b, CB_K + h)),
            pl.BlockSpec((SEQ, DH_B), lambda b, h: (b, CB_V + h)),
        ],
        out_specs=pl.BlockSpec((SEQ, DH_B), lambda b, h: (b, h)),
        out_shape=jax.ShapeDtypeStruct((T_CTX, BRANCH_W), F32),
        compiler_params=_params(("arbitrary", "arbitrary")),
        name="context_attention",
    )(proj, proj, proj)


def _na_kernel(q_ref, k_ref, v_ref, kc_ref, vc_ref, cos_ref, sin_ref, bias_ref, o_ref,
               qr_ref, qn_ref, kr_ref, vb_ref):
    scale = DH_B ** -0.5
    rows = DEC_SEQ // GRID_W
    kh = MAX_KH
    cos = cos_ref[...]
    sin = sin_ref[...]
    q = q_ref[...]
    k = k_ref[...]
    qr_ref[...] = (q * cos + pltpu.roll(q, DH_B // 2, 1) * sin).astype(BF16)
    kr_ref[...] = (k * cos + pltpu.roll(k, DH_B // 2, 1) * sin).astype(BF16)
    qn_ref[...] = q.astype(BF16)
    vb_ref[...] = v_ref[...].astype(BF16)
    kc = kc_ref[0, 0].astype(BF16)
    vc = vc_ref[0, 0].astype(BF16)

    def row_step(r, carry):
        start = jnp.clip(r - kh // 2, 0, rows - kh)
        qrow = pl.ds(pl.multiple_of(r * GRID_W, GRID_W), GRID_W)
        krow = pl.ds(pl.multiple_of(start * GRID_W, GRID_W), kh * GRID_W)
        s_loc = _dot_nt(qr_ref[qrow, :], kr_ref[krow, :]) * scale + bias_ref[r - start, 0]
        s_ctx = _dot_nt(qn_ref[qrow, :], kc) * scale
        m = jnp.maximum(jnp.max(s_loc, axis=-1, keepdims=True), jnp.max(s_ctx, axis=-1, keepdims=True))
        p_loc = jnp.exp(s_loc - m)
        p_ctx = jnp.exp(s_ctx - m)
        den = jnp.sum(p_loc, axis=-1, keepdims=True) + jnp.sum(p_ctx, axis=-1, keepdims=True)
        o = _dot(p_loc.astype(BF16), vb_ref[krow, :]) + _dot(p_ctx.astype(BF16), vc)
        o_ref[qrow, :] = o / den
        return carry

    lax.fori_loop(0, rows, row_step, 0)


def _neighbourhood_attention(proj, cache_k, cache_v, cos_t, sin_t, bias_tab, layer):
    ck = cache_k.reshape(DEC_BATCH, DEPTH, PAST_LEN, H_B * DH_B)
    cv = cache_v.reshape(DEC_BATCH, DEPTH, PAST_LEN, H_B * DH_B)
    lat = T_CTX // DEC_SEQ
    return pl.pallas_call(
        _na_kernel,
        grid=(DEC_BATCH, H_B),
        in_specs=[
            pl.BlockSpec((DEC_SEQ, DH_B), lambda b, h: (lat + b, CB_Q + h)),
            pl.BlockSpec((DEC_SEQ, DH_B), lambda b, h: (lat + b, CB_K + h)),
            pl.BlockSpec((DEC_SEQ, DH_B), lambda b, h: (lat + b, CB_V + h)),
            pl.BlockSpec((1, 1, PAST_LEN, DH_B), lambda b, h: (b, layer, 0, h)),
            pl.BlockSpec((1, 1, PAST_LEN, DH_B), lambda b, h: (b, layer, 0, h)),
            pl.BlockSpec((DEC_SEQ, DH_B), lambda b, h: (0, 0)),
            pl.BlockSpec((DEC_SEQ, DH_B), lambda b, h: (0, 0)),
            pl.BlockSpec((MAX_KH, 1, GRID_W, MAX_KH * GRID_W), lambda b, h: (0, h, 0, 0)),
        ],
        out_specs=pl.BlockSpec((DEC_SEQ, DH_B), lambda b, h: (b, h)),
        out_shape=jax.ShapeDtypeStruct((T_LAT, BRANCH_W), F32),
        scratch_shapes=[pltpu.VMEM((DEC_SEQ, DH_B), BF16) for _ in range(4)],
        compiler_params=_params(("arbitrary", "arbitrary")),
        name="neighbourhood_attention",
    )(proj, proj, proj, ck, cv, cos_t, sin_t, bias_tab)


def _ref_rows(a, blk, r):
    pieces = [jnp.broadcast_to(a[b0 + r:b0 + r + 1, :], (blk, a.shape[1])) for b0 in range(0, HG_CHUNK, blk)]
    return pieces[0] if len(pieces) == 1 else jnp.concatenate(pieces, axis=0)


def _split3(x):
    hi = x.astype(BF16)
    r1 = x - hi.astype(F32)
    mid = r1.astype(BF16)
    lo = (r1 - mid.astype(F32)).astype(BF16)
    return hi, mid, lo


def _hgrn_chunk(q, z, v, lb, tri, masks, st, backward):
    c = HG_CHUNK
    f = jnp.maximum(lb + (1.0 - lb) * _sigmoid(z), F_MIN)
    gl = jnp.log(f)
    kk = 1.0 - f
    hi, mid, lo = _split3(gl)
    a = _dot(tri, hi) + _dot(tri, mid) + _dot(tri, lo)

    d0 = a - _ref_rows(a, HG_LEVELS[0], HG_LEVELS[0] // 2 if backward else HG_LEVELS[0] // 2 - 1)
    qw = (q * jnp.exp(d0)).astype(BF16)
    kw = (kk * jnp.exp(-d0)).astype(BF16)
    scores = _dot_nt(qw, kw) * masks[0]
    for li, blk in enumerate(HG_LEVELS[1:], start=1):
        dl = a - _ref_rows(a, blk, blk // 2 if backward else blk // 2 - 1)
        w = jnp.exp(-jnp.abs(dl))
        scores = scores + _dot_nt((q * w).astype(BF16), (kk * w).astype(BF16)) * masks[li]
    vb = v.astype(BF16)
    o = _dot(scores.astype(BF16), vb)
    o = o + _dot_nt((q * jnp.exp(a)).astype(BF16), st.astype(BF16))
    a_tot = a[0:1, :] if backward else a[c - 1:c, :]
    kd = (kk * jnp.exp(a_tot - a)).astype(BF16)
    st_new = st * jnp.exp(a_tot) + _dot_tn(vb, kd)
    return o, st_new


def _hgrn_kernel(*refs, seq_len, zero_init, emit_state):
    q_ref, zf_ref, zb_ref, v_ref, g_ref, lb_ref, nw_ref, tri_ref, msk_ref = refs[:9]
    pos = 9
    s0_ref = None
    if not zero_init:
        s0_ref = refs[pos]
        pos += 1
    o_ref = refs[pos]
    pos += 1
    sfin_ref = None
    if emit_state:
        sfin_ref = refs[pos]
        pos += 1
    oacc_ref, st_ref = refs[pos], refs[pos + 1]
    nchunk = seq_len // HG_CHUNK
    nw = nw_ref[0]

    for d in (0, 1):
        backward = d == 1
        z_ref = zb_ref if backward else zf_ref
        lb = lb_ref[0, d, 0]
        tri = tri_ref[d]
        if zero_init:
            st_ref[...] = jnp.zeros((DK, DK), F32)
        else:
            st_ref[...] = s0_ref[0, 0, d, 0].T

        def body(ci, carry, backward=backward, z_ref=z_ref, lb=lb, tri=tri, d=d):
            cidx = nchunk - 1 - ci if backward else ci
            rows = pl.ds(pl.multiple_of(cidx * HG_CHUNK, HG_CHUNK), HG_CHUNK)
            masks = [msk_ref[d, li] for li in range(len(HG_LEVELS))]
            o, st_new = _hgrn_chunk(q_ref[rows, :], z_ref[rows, :], v_ref[rows, :], lb, tri, masks,
                                    st_ref[...], backward)
            st_ref[...] = st_new
            if not backward:
                oacc_ref[rows, :] = o
            else:
                o = o + oacc_ref[rows, :]
                o = o * lax.rsqrt(jnp.mean(o * o, axis=-1, keepdims=True) + LN_EPS) * nw
                o_ref[rows, :] = o * _silu(g_ref[rows, :])
            return carry

        lax.fori_loop(0, nchunk, body, 0)
        if emit_state:
            sfin_ref[0, d, 0] = st_ref[...].T


def _hgrn_consts():
    c = HG_CHUNK
    t = jnp.arange(c)[:, None]
    s = jnp.arange(c)[None, :]
    tril = (s <= t)
    tri = jnp.stack([tril, tril.T]).astype(BF16)
    fwd = [(t // HG_LEVELS[0] == s // HG_LEVELS[0]) & (s <= t)]
    for blk in HG_LEVELS[1:]:
        fwd.append((t // blk == s // blk) & (t % blk >= blk // 2) & (s % blk < blk // 2))
    fwd = jnp.stack(fwd)
    masks = jnp.stack([fwd, jnp.swapaxes(fwd, 1, 2)]).astype(F32)
    return tri, masks


def _hgrn_mixer(proj, lower, norm_w, tri, masks, layer, *, n_seq, seq_len, row_block0, s0):
    zero_init = s0 is None
    seq_spec = lambda cb: pl.BlockSpec((seq_len, DK), lambda b, h: (row_block0 + b, cb + h))
    in_specs = [
        seq_spec(CB_HQ), seq_spec(CB_HFF), seq_spec(CB_HFB), seq_spec(CB_HI), seq_spec(CB_HG),
        pl.BlockSpec((1, 2, 1, 1, DK), lambda b, h: (layer, 0, h, 0, 0)),
        pl.BlockSpec((1, 1, DK), lambda b, h: (layer, 0, 0)),
        pl.BlockSpec((2, HG_CHUNK, HG_CHUNK), lambda b, h: (0, 0, 0)),
        pl.BlockSpec((2, len(HG_LEVELS), HG_CHUNK, HG_CHUNK), lambda b, h: (0, 0, 0, 0)),
    ]
    args = [proj, proj, proj, proj, proj, lower, norm_w, tri, masks]
    if not zero_init:
        in_specs.append(pl.BlockSpec((1, 1, 2, 1, DK, DK), lambda b, h: (b, layer, 0, h, 0, 0)))
        args.append(s0)
    out_specs = [pl.BlockSpec((seq_len, DK), lambda b, h: (b, h))]
    out_shape = [jax.ShapeDtypeStruct((n_seq * seq_len, BRANCH_W), F32)]
    if zero_init:
        out_specs.append(pl.BlockSpec((1, 2, 1, DK, DK), lambda b, h: (b, 0, h, 0, 0)))
        out_shape.append(jax.ShapeDtypeStruct((n_seq, 2, H_C, DK, DK), F32))
    res = pl.pallas_call(
        functools.partial(_hgrn_kernel, seq_len=seq_len, zero_init=zero_init, emit_state=zero_init),
        grid=(n_seq, H_C),
        in_specs=in_specs,
        out_specs=out_specs,
        out_shape=out_shape,
        scratch_shapes=[pltpu.VMEM((seq_len, DK), F32), pltpu.VMEM((DK, DK), F32)],
        compiler_params=_params(("arbitrary", "arbitrary")),
        name="hgrn2_ctx" if zero_init else "hgrn2_latent",
    )(*args)
    return res if zero_init else (res[0], None)


def _merge_kernel(ya_ref, ybc_ref, ybl_ref, ycc_ref, ycl_ref, ga_ref, gb_ref, gc_ref, w_ref, o_ref, *, n_ctx_tiles):
    is_ctx = pl.program_id(1) < n_ctx_tiles
    yb = jnp.where(is_ctx, ybc_ref[...], ybl_ref[...])
    yc = jnp.where(is_ctx, ycc_ref[...], ycl_ref[...])
    acc = _sigmoid(ga_ref[...]) * _dot(ya_ref[...].astype(BF16), w_ref[0, 0])
    acc = acc + _sigmoid(gb_ref[...]) * _dot(yb.astype(BF16), w_ref[0, 1])
    acc = acc + _sigmoid(gc_ref[...]) * _dot(yc.astype(BF16), w_ref[0, 2])
    o_ref[...] = acc.astype(BF16)


def _merge_branches(ya, yb_ctx, yb_lat, yc_ctx, yc_lat, proj, wb_bf, layer):
    tm, tn = 256, 1024
    nj = D_MODEL // tn
    nc = T_CTX // tm
    gate0 = CB_GATES * LANE // tn
    y_spec = pl.BlockSpec((tm, BRANCH_W), lambda j, i: (i, 0))
    ctx_spec = pl.BlockSpec((tm, BRANCH_W), lambda j, i: (jnp.minimum(i, nc - 1), 0))
    lat_spec = pl.BlockSpec((tm, BRANCH_W), lambda j, i: (jnp.maximum(i - nc, 0), 0))
    gate_spec = lambda k: pl.BlockSpec((tm, tn), lambda j, i: (i, gate0 + k * nj + j))
    return pl.pallas_call(
        functools.partial(_merge_kernel, n_ctx_tiles=nc),
        grid=(nj, T_ALL // tm),
        in_specs=[y_spec, ctx_spec, lat_spec, ctx_spec, lat_spec, gate_spec(0), gate_spec(1), gate_spec(2),
                  pl.BlockSpec((1, 3, BRANCH_W, tn), lambda j, i: (layer, 0, 0, j))],
        out_specs=pl.BlockSpec((tm, tn), lambda j, i: (i, j)),
        out_shape=jax.ShapeDtypeStruct((T_ALL, D_MODEL), BF16),
        compiler_params=_params(("arbitrary", "arbitrary")),
        name="merge_branches",
    )(ya, yb_ctx, yb_lat, yc_ctx, yc_lat, proj, proj, proj, wb_bf)


def _out_router_kernel(mg_ref, x_ref, mod_ref, wo_ref, lnw_ref, lnb_ref, rwh_ref, rwl_ref, rb_ref,
                       x1_ref, h2_ref, rt_ref, *, alpha):
    mix = _dot(mg_ref[...], wo_ref[0])
    g1 = mod_ref[0, 0, 2:3, :]
    x1 = _layer_norm_rows(alpha * x_ref[...] + g1 * mix, lnw_ref[0], lnb_ref[0])
    x1_ref[...] = x1
    h2 = x1 * (1.0 + mod_ref[0, 0, 4:5, :]) + mod_ref[0, 0, 3:4, :]
    h2_ref[...] = h2

    hh = h2.astype(BF16)
    hl = (h2 - hh.astype(F32)).astype(BF16)
    lg = _dot(hh, rwh_ref[0]) + _dot(hl, rwh_ref[0]) + _dot(hh, rwl_ref[0]) + rb_ref[0]

    lane = lax.broadcasted_iota(jnp.int32, lg.shape, 1)
    neg = jnp.float32(-jnp.inf)
    big = jnp.int32(1 << 20)
    is_c = lane < N_GROUPS
    cmax = jnp.max(jnp.where(is_c, lg, neg), axis=-1, keepdims=True)
    g_idx = jnp.min(jnp.where(is_c & (lg == cmax), lane, big), axis=-1, keepdims=True)
    p_group = 1.0 / jnp.sum(jnp.where(is_c, jnp.exp(lg - cmax), 0.0), axis=-1, keepdims=True)
    lo_lane = N_GROUPS + EXPERTS_PER_GROUP * g_idx
    in_g = (lane >= lo_lane) & (lane < lo_lane + EXPERTS_PER_GROUP)
    v1 = jnp.max(jnp.where(in_g, lg, neg), axis=-1, keepdims=True)
    i1 = jnp.min(jnp.where(in_g & (lg == v1), lane, big), axis=-1, keepdims=True)
    rest = in_g & (lane != i1)
    v2 = jnp.max(jnp.where(rest, lg, neg), axis=-1, keepdims=True)
    i2 = jnp.min(jnp.where(rest & (lg == v2), lane, big), axis=-1, keepdims=True)
    e21 = jnp.exp(v2 - v1)
    w1 = p_group / (1.0 + e21)
    w2 = p_group * e21 / (1.0 + e21)
    e1 = (i1 - N_GROUPS).astype(F32)
    e2 = (i2 - N_GROUPS).astype(F32)
    rt_ref[...] = jnp.where(lane == 0, e1, jnp.where(lane == 1, e2, jnp.where(lane == 2, w1, jnp.where(lane == 3, w2, 0.0))))


def _out_proj_router(merged, x, mods, wo_bf, ln_w, ln_b, rw_hi, rw_lo, rb, layer, alpha):
    tm = 256
    row = pl.BlockSpec((tm, D_MODEL), lambda i: (i, 0))
    vec = pl.BlockSpec((1, 1, D_MODEL), lambda i: (layer, 0, 0))
    rw_spec = pl.BlockSpec((1, D_MODEL, ROUTE_LANES), lambda i: (layer, 0, 0))
    return pl.pallas_call(
        functools.partial(_out_router_kernel, alpha=alpha),
        grid=(T_ALL // tm,),
        in_specs=[
            row, row,
            pl.BlockSpec((1, 1, 6, D_MODEL), lambda i: (layer, (i * tm) // SLAB, 0, 0)),
            pl.BlockSpec((1, D_MODEL, D_MODEL), lambda i: (layer, 0, 0)),
            vec, vec, rw_spec, rw_spec,
            pl.BlockSpec((1, 1, ROUTE_LANES), lambda i: (layer, 0, 0)),
        ],
        out_specs=[row, row, pl.BlockSpec((tm, ROUTE_LANES), lambda i: (i, 0))],
        out_shape=[jax.ShapeDtypeStruct((T_ALL, D_MODEL), F32), jax.ShapeDtypeStruct((T_ALL, D_MODEL), F32),
                   jax.ShapeDtypeStruct((T_ALL, ROUTE_LANES), F32)],
        compiler_params=_params(("arbitrary",)),
        name="out_proj_router",
    )(merged, x, mods, wo_bf, ln_w, ln_b, rw_hi, rw_lo, rb)


def _row_gather(src_hbm, idx_of, dst_ref, sem, n_rows):
    def issue(r, carry):
        pltpu.make_async_copy(src_hbm.at[pl.ds(idx_of(r), 1)], dst_ref.at[pl.ds(r, 1)], sem).start()
        return carry
    lax.fori_loop(0, n_rows, issue, 0)


def _row_gather_wait(src_hbm, dst_ref, sem, n_rows):
    pltpu.make_async_copy(src_hbm.at[pl.ds(0, n_rows)], dst_ref, sem).wait()


def _moe_kernel(be_ref, nused_ref, tok_ref, h2_hbm, w1_ref, w3_ref, w2_ref, rw_ref, o_ref, xbuf, sem):
    del be_ref
    i = pl.program_id(0)

    @pl.when(i < nused_ref[0])
    def _():
        _row_gather(h2_hbm, lambda r: tok_ref[0, 0, r], xbuf, sem, MOE_BLK)
        _row_gather_wait(h2_hbm, xbuf, sem, MOE_BLK)
        xb = xbuf[...].astype(BF16)
        hid = _silu(_dot(xb, w1_ref[0, 0])) * _dot(xb, w3_ref[0, 0])
        o_ref[...] = _dot(hid.astype(BF16), w2_ref[0, 0]) * rw_ref[...]

    @pl.when(i >= nused_ref[0])
    def _():
        o_ref[...] = jnp.zeros(o_ref.shape, F32)


def _routed_experts(h2, block_expert, n_used, row_tok, row_w, w1_bf, w3_bf, w2_bf, layer):
    grid_spec = pltpu.PrefetchScalarGridSpec(
        num_scalar_prefetch=2,
        grid=(N_MOE_BLOCKS,),
        in_specs=[
            pl.BlockSpec((1, 1, MOE_BLK), lambda i, be, nu: (i, 0, 0), memory_space=pltpu.SMEM),
            pl.BlockSpec(memory_space=pl.ANY),
            pl.BlockSpec((1, 1, D_MODEL, D_EXPERT), lambda i, be, nu: (layer, be[i], 0, 0)),
            pl.BlockSpec((1, 1, D_MODEL, D_EXPERT), lambda i, be, nu: (layer, be[i], 0, 0)),
            pl.BlockSpec((1, 1, D_EXPERT, D_MODEL), lambda i, be, nu: (layer, be[i], 0, 0)),
            pl.BlockSpec((MOE_BLK, 1), lambda i, be, nu: (i, 0)),
        ],
        out_specs=pl.BlockSpec((MOE_BLK, D_MODEL), lambda i, be, nu: (i, 0)),
        scratch_shapes=[pltpu.VMEM((MOE_BLK, D_MODEL), F32), pltpu.SemaphoreType.DMA(())],
    )
    return pl.pallas_call(
        _moe_kernel,
        grid_spec=grid_spec,
        out_shape=jax.ShapeDtypeStruct((N_MOE_ROWS, D_MODEL), F32),
        compiler_params=_params(("arbitrary",)),
        name="routed_experts",
    )(block_expert, n_used, row_tok.reshape(N_MOE_BLOCKS, 1, MOE_BLK), h2, w1_bf, w3_bf, w2_bf,
      row_w.reshape(N_MOE_ROWS, 1))


def _combine_kernel(pos_ref, x1_ref, mod_ref, lnw_ref, lnb_ref, yb_hbm, o_ref, buf0, buf1, sem0, sem1, *, alpha, tm):
    _row_gather(yb_hbm, lambda r: pos_ref[0, 0, r], buf0, sem0, tm)
    _row_gather(yb_hbm, lambda r: pos_ref[0, 1, r], buf1, sem1, tm)
    _row_gather_wait(yb_hbm, buf0, sem0, tm)
    _row_gather_wait(yb_hbm, buf1, sem1, tm)
    g2 = mod_ref[0, 0, 5:6, :]
    y = alpha * x1_ref[...] + g2 * (buf0[...] + buf1[...])
    o_ref[...] = _layer_norm_rows(y, lnw_ref[0], lnb_ref[0])


def _combine_norm(pos, x1, mods, ln_w, ln_b, yb, layer, alpha):
    tm = 256
    row = pl.BlockSpec((tm, D_MODEL), lambda i: (i, 0))
    vec = pl.BlockSpec((1, 1, D_MODEL), lambda i: (layer, 0, 0))
    return pl.pallas_call(
        functools.partial(_combine_kernel, alpha=alpha, tm=tm),
        grid=(T_ALL // tm,),
        in_specs=[
            pl.BlockSpec((1, TOP_K, tm), lambda i: (i, 0, 0), memory_space=pltpu.SMEM),
            row,
            pl.BlockSpec((1, 1, 6, D_MODEL), lambda i: (layer, (i * tm) // SLAB, 0, 0)),
            vec, vec,
            pl.BlockSpec(memory_space=pl.ANY),
        ],
        out_specs=row,
        out_shape=jax.ShapeDtypeStruct((T_ALL, D_MODEL), F32),
        scratch_shapes=[pltpu.VMEM((tm, D_MODEL), F32), pltpu.VMEM((tm, D_MODEL), F32),
                        pltpu.SemaphoreType.DMA(()), pltpu.SemaphoreType.DMA(())],
        compiler_params=_params(("arbitrary",)),
        name="combine_norm",
    )(pos, x1, mods, ln_w, ln_b, yb)


def _dispatch_plan(route):
    ids = route[:, 0:TOP_K].astype(jnp.int32)
    wts = route[:, TOP_K:2 * TOP_K]
    flat_e = ids.reshape(-1)
    flat_w = wts.reshape(-1)
    onehot = (flat_e[:, None] == jnp.arange(N_EXPERTS, dtype=jnp.int32)[None, :]).astype(jnp.int32)
    csum = jnp.cumsum(onehot, axis=0)
    rank = jnp.sum(csum * onehot, axis=1) - 1
    counts = csum[-1]
    padded = (counts + MOE_BLK - 1) // MOE_BLK * MOE_BLK
    pad_end = jnp.cumsum(padded)
    pad_start = pad_end - padded
    dest = (jnp.sum(onehot * pad_start[None, :], axis=1) + rank).astype(jnp.int32)
    tok = jnp.arange(T_ALL * TOP_K, dtype=jnp.int32) // TOP_K
    row_tok = jnp.zeros((N_MOE_ROWS,), jnp.int32).at[dest].set(tok)
    row_w = jnp.zeros((N_MOE_ROWS,), F32).at[dest].set(flat_w)
    block_expert = jnp.clip(jnp.searchsorted(pad_end, jnp.arange(N_MOE_BLOCKS, dtype=jnp.int32) * MOE_BLK, side='right'),
                            0, N_EXPERTS - 1).astype(jnp.int32)
    n_used = (pad_end[-1] // MOE_BLK).astype(jnp.int32).reshape(1)
    tm = 256
    pos = dest.reshape(T_ALL // tm, tm, TOP_K).transpose(0, 2, 1)
    return block_expert, n_used, row_tok, row_w, pos


def _rope_tables():
    pos = jnp.arange(DEC_SEQ)
    row = (pos // GRID_W).astype(F32)
    col = (pos % GRID_W).astype(F32)
    n_freq = DH_B // 4
    inv = ROPE_BASE ** (-jnp.arange(n_freq, dtype=F32) / n_freq)
    ang = jnp.concatenate([row[:, None] * inv, col[:, None] * inv], axis=-1)
    cos, sin = jnp.cos(ang), jnp.sin(ang)
    return jnp.concatenate([cos, cos], axis=-1), jnp.concatenate([-sin, sin], axis=-1)


def _na_bias_table(rpb):
    cols = jnp.arange(GRID_W)
    col_start = jnp.clip(cols - KW // 2, 0, GRID_W - KW)
    col_mask = (cols[None, :] >= col_start[:, None]) & (cols[None, :] < col_start[:, None] + KW)
    dx_idx = jnp.clip(cols[None, :] - cols[:, None] + KW - 1, 0, 2 * KW - 2)
    u = jnp.arange(MAX_KH)[:, None]
    i = jnp.arange(MAX_KH)[None, :]
    dy = i - u + MAX_KH - 1
    bias = rpb[:, dy[:, :, None, None], dx_idx[None, None, :, :]]
    bias = jnp.where(col_mask[None, None, None], bias, MASK_VALUE)
    bias = bias.transpose(1, 0, 3, 2, 4)
    return bias.reshape(MAX_KH, H_B, GRID_W, MAX_KH * GRID_W)


def kernel(x_prompt, x_sample, cache_k, cache_v, state_hgrn, c, c_ctx, w_mod, b_mod, w_in, gmlp_ws, gmlp_bs, na_rpb, hgrn_lb, hgrn_norm_w, w_branch, w_o, ln1_w, ln1_b, ln2_w, ln2_b, router_w1, router_b1, router_w2, router_b2, moe_w1, moe_w3, moe_w2):
    alpha = (2.0 * DEPTH) ** 0.25
    x = jnp.concatenate([x_prompt.reshape(T_CTX, D_MODEL), x_sample.reshape(T_LAT, D_MODEL)], axis=0)
    cond = jnp.concatenate([c_ctx[None, :], c, jnp.zeros((MOD_ROWS - 1 - DEC_BATCH, D_MODEL), F32)], axis=0)
    mods = _modulation(cond, w_mod, b_mod)

    p_lb = jax.nn.softmax(hgrn_lb.astype(F32), axis=0)
    lower = (jnp.cumsum(p_lb, axis=0) - p_lb[0:1]).reshape(DEPTH, 2, H_C, 1, DK)
    norm_w = hgrn_norm_w.reshape(DEPTH, 1, DK)
    tri, masks = _hgrn_consts()
    cos_t, sin_t = _rope_tables()

    ws_bf = gmlp_ws.astype(BF16)
    bs_b = jnp.repeat(jnp.swapaxes(gmlp_bs, 1, 2), LANE, axis=2)
    wb_bf = w_branch.astype(BF16)
    wo_bf = w_o.astype(BF16)
    rw = jnp.concatenate([router_w1, router_w2.transpose(0, 2, 1, 3).reshape(DEPTH, D_MODEL, N_EXPERTS),
                          jnp.zeros((DEPTH, D_MODEL, ROUTE_LANES - N_GROUPS - N_EXPERTS), F32)], axis=-1)
    rw_hi = rw.astype(BF16)
    rw_lo = (rw - rw_hi.astype(F32)).astype(BF16)
    rb = jnp.concatenate([router_b1, router_b2.reshape(DEPTH, N_EXPERTS),
                          jnp.zeros((DEPTH, ROUTE_LANES - N_GROUPS - N_EXPERTS), F32)], axis=-1).reshape(DEPTH, 1, ROUTE_LANES)
    w1_bf = moe_w1.astype(BF16)
    w3_bf = moe_w3.astype(BF16)
    w2_bf = moe_w2.astype(BF16)
    ln1w, ln1b = ln1_w.reshape(DEPTH, 1, D_MODEL), ln1_b.reshape(DEPTH, 1, D_MODEL)
    ln2w, ln2b = ln2_w.reshape(DEPTH, 1, D_MODEL), ln2_b.reshape(DEPTH, 1, D_MODEL)

    ks_out, vs_out, ss_out = [], [], []
    for l in range(DEPTH):
        proj = _in_projection(x, mods, w_in, l)
        ya = _gmlp_mixer(proj, ws_bf, bs_b, l)
        yb_ctx = _context_attention(proj)
        yb_lat = _neighbourhood_attention(proj, cache_k, cache_v, cos_t, sin_t, _na_bias_table(na_rpb[l]), l)
        yc_ctx, s_ctx = _hgrn_mixer(proj, lower, norm_w, tri, masks, l,
                                    n_seq=BATCH, seq_len=SEQ, row_block0=0, s0=None)
        yc_lat, _ = _hgrn_mixer(proj, lower, norm_w, tri, masks, l,
                                n_seq=DEC_BATCH, seq_len=DEC_SEQ, row_block0=T_CTX // DEC_SEQ, s0=state_hgrn)
        merged = _merge_branches(ya, yb_ctx, yb_lat, yc_ctx, yc_lat, proj, wb_bf, l)
        x1, h2, route = _out_proj_router(merged, x, mods, wo_bf, ln1w, ln1b, rw_hi, rw_lo, rb, l, alpha)
        block_expert, n_used, row_tok, row_w, pos = _dispatch_plan(route)
        ybk = _routed_experts(h2, block_expert, n_used, row_tok, row_w, w1_bf, w3_bf, w2_bf, l)
        x = _combine_norm(pos, x1, mods, ln2w, ln2b, ybk, l, alpha)

        kv = proj[:T_CTX, CB_K * LANE:(CB_V + H_B) * LANE]
        ks_out.append(kv[:, :BRANCH_W].reshape(BATCH, SEQ, H_B, DH_B))
        vs_out.append(kv[:, BRANCH_W:].reshape(BATCH, SEQ, H_B, DH_B))
        ss_out.append(s_ctx)

    y_prompt = x[:T_CTX].reshape(BATCH, SEQ, D_MODEL)
    y_sample = x[T_CTX:].reshape(DEC_BATCH, DEC_SEQ, D_MODEL)
    return (y_prompt, y_sample, jnp.stack(ks_out, axis=1), jnp.stack(vs_out, axis=1), jnp.stack(ss_out, axis=1))
```

```python
import functools

import jax
import jax.numpy as jnp
from jax import lax
from jax.experimental import pallas as pl
from jax.experimental.pallas import tpu as pltpu

F32 = jnp.float32
BF16 = jnp.bfloat16

D_MODEL = 2048
BATCH = 16
SEQ = 256
DEPTH = 2
DEC_BATCH = 4
DEC_SEQ = 4096
PAST_LEN = 512
GRID_W = 64
BRANCH_W = D_MODEL // 2
CHUNK_A = 128
G_A = 8
DH_B = 128
H_B = 8
MAX_KH = 8
KW = 16
ROPE_BASE = 10000.0
DK = 128
H_C = 8
N_GROUPS = 4
EXPERTS_PER_GROUP = 8
N_EXPERTS = 32
TOP_K = 2
D_EXPERT = D_MODEL // 2
LN_EPS = 1e-5
MASK_VALUE = -1e30
F_MIN = 1e-6
IN_COLS = 16384

T_CTX = BATCH * SEQ
T_LAT = DEC_BATCH * DEC_SEQ
T_ALL = T_CTX + T_LAT
SLAB = 4096
N_SLAB = T_ALL // SLAB
MOD_ROWS = 8
LANE = 128
CB_Q, CB_K, CB_V = 16, 24, 32
CB_HQ, CB_HFF, CB_HFB, CB_HI, CB_HG = 40, 48, 56, 64, 72
CB_GATES = 80

HG_CHUNK = 128
HG_LEVELS = (8, 16, 32, 64, 128)
HG_FIN_TILE = 256
NA_CTX_TILE = 512
NA_GROUP = 4
NA_UNION = NA_GROUP + MAX_KH
MOE_BLK = 256
N_MOE_ROWS = T_ALL * TOP_K + N_EXPERTS * MOE_BLK
N_MOE_BLOCKS = N_MOE_ROWS // MOE_BLK
ROUTE_LANES = 128
COMBINE_TILE = 256
ROW_ISSUE_UNROLL = 8

VMEM_LIMIT = 56 * 1024 * 1024


def _params(sem):
    return pltpu.CompilerParams(dimension_semantics=sem, vmem_limit_bytes=VMEM_LIMIT)


def _sigmoid(x):
    return 1.0 / (1.0 + jnp.exp(-x))


def _silu(x):
    return x * _sigmoid(x)


def _gelu_tanh(x):
    return x * (0.5 * (1.0 + jnp.tanh(0.7978845608028654 * (x + 0.044715 * (x * x * x)))))


def _layer_norm_rows(y, w, b):
    mu = jnp.mean(y, axis=-1, keepdims=True)
    yc = y - mu
    var = jnp.mean(yc * yc, axis=-1, keepdims=True)
    return yc * lax.rsqrt(var + LN_EPS) * w + b


def _dot(a, b):
    return jnp.dot(a, b, preferred_element_type=F32)


def _dot_nt(a, b):
    return lax.dot_general(a, b, (((1,), (1,)), ((), ())), preferred_element_type=F32)


def _dot_tn(a, b):
    return lax.dot_general(a, b, (((0,), (0,)), ((), ())), preferred_element_type=F32)


def _mod_kernel(c_ref, w_ref, b_ref, o_ref):
    o_ref[0] = _dot(_silu(c_ref[...]), w_ref[0]) + b_ref[0]


def _modulation(cond, w_mod, b_mod):
    tn = 1024
    n = 6 * D_MODEL
    out = pl.pallas_call(
        _mod_kernel,
        grid=(DEPTH, n // tn),
        in_specs=[
            pl.BlockSpec((MOD_ROWS, D_MODEL), lambda l, j: (0, 0)),
            pl.BlockSpec((1, D_MODEL, tn), lambda l, j: (l, 0, j)),
            pl.BlockSpec((1, 1, tn), lambda l, j: (l, 0, j)),
        ],
        out_specs=pl.BlockSpec((1, MOD_ROWS, tn), lambda l, j: (l, 0, j)),
        out_shape=jax.ShapeDtypeStruct((DEPTH, MOD_ROWS, n), F32),
        compiler_params=_params(("arbitrary", "arbitrary")),
        name="modulation",
    )(cond, w_mod, b_mod.reshape(DEPTH, 1, n))
    return out.reshape(DEPTH, MOD_ROWS, 6, D_MODEL)


def _inproj_kernel(x_ref, mod_ref, w_ref, o_ref, wbf_ref):
    @pl.when(pl.program_id(1) == 0)
    def _():
        wbf_ref[...] = w_ref[0].astype(BF16)

    h = x_ref[...] * (1.0 + mod_ref[0, 0, 1:2, :]) + mod_ref[0, 0, 0:1, :]
    o_ref[...] = _dot(h.astype(BF16), wbf_ref[...])


def _in_projection(x, mods, w_in, layer):
    tm, tn = 512, 1024
    return pl.pallas_call(
        _inproj_kernel,
        grid=(IN_COLS // tn, T_ALL // tm),
        in_specs=[
            pl.BlockSpec((tm, D_MODEL), lambda j, i: (i, 0)),
            pl.BlockSpec((1, 1, 6, D_MODEL), lambda j, i: (layer, (i * tm) // SLAB, 0, 0)),
            pl.BlockSpec((1, D_MODEL, tn), lambda j, i: (layer, 0, j)),
        ],
        out_specs=pl.BlockSpec((tm, tn), lambda j, i: (i, j)),
        out_shape=jax.ShapeDtypeStruct((T_ALL, IN_COLS), F32),
        scratch_shapes=[pltpu.VMEM((D_MODEL, tn), BF16)],
        compiler_params=_params(("arbitrary", "arbitrary")),
        name="in_projection",
    )(x, mods, w_in)


def _gmlp_kernel(uv_ref, ws_ref, bs_ref, o_ref, *, tm):
    ge = _gelu_tanh(uv_ref[...])
    u = ge[:, :BRANCH_W]
    v = ge[:, BRANCH_W:]
    mu = jnp.mean(v, axis=-1, keepdims=True)
    vc = v - mu
    var = jnp.mean(vc * vc, axis=-1, keepdims=True)
    vn = (vc * lax.rsqrt(var + LN_EPS)).astype(BF16)
    for n in range(tm // CHUNK_A):
        r = slice(n * CHUNK_A, (n + 1) * CHUNK_A)
        for g in range(G_A):
            c = slice(g * LANE, (g + 1) * LANE)
            sv = _dot(ws_ref[0, g], vn[r, c]) + bs_ref[0, :, c]
            o_ref[r, c] = u[r, c] * sv


def _gmlp_mixer(proj, ws_bf, bs_b, layer):
    tm = 256
    return pl.pallas_call(
        functools.partial(_gmlp_kernel, tm=tm),
        grid=(T_ALL // tm,),
        in_specs=[
            pl.BlockSpec((tm, 2 * BRANCH_W), lambda i: (i, 0)),
            pl.BlockSpec((1, G_A, CHUNK_A, CHUNK_A), lambda i: (layer, 0, 0, 0)),
            pl.BlockSpec((1, CHUNK_A, BRANCH_W), lambda i: (layer, 0, 0)),
        ],
        out_specs=pl.BlockSpec((tm, BRANCH_W), lambda i: (i, 0)),
        out_shape=jax.ShapeDtypeStruct((T_ALL, BRANCH_W), F32),
        compiler_params=_params(("arbitrary",)),
        name="gmlp_mixer",
    )(proj, ws_bf, bs_b)


def _ctx_attn_kernel(q_ref, k_ref, v_ref, o_ref):
    scale = DH_B ** -0.5
    s = _dot_nt(q_ref[...].astype(BF16), k_ref[...].astype(BF16)) * scale
    m = jnp.max(s, axis=-1, keepdims=True)
    p = jnp.exp(s - m)
    den = jnp.sum(p, axis=-1, keepdims=True)
    o_ref[...] = _dot(p.astype(BF16), v_ref[...].astype(BF16)) / den


def _context_attention(proj):
    return pl.pallas_call(
        _ctx_attn_kernel,
        grid=(BATCH, H_B),
        in_specs=[
            pl.BlockSpec((SEQ, DH_B), lambda b, h: (b, CB_Q + h)),
            pl.BlockSpec((SEQ, DH_B), lambda b, h: (b, CB_K + h)),
            pl.BlockSpec((SEQ, DH_B), lambda b, h: (b, CB_V + h)),
        ],
        out_specs=pl.BlockSpec((SEQ, DH_B), lambda b, h: (b, h)),
        out_shape=jax.ShapeDtypeStruct((T_CTX, BRANCH_W), F32),
        compiler_params=_params(("arbitrary", "arbitrary")),
        name="context_attention",
    )(proj, proj, proj)


def _na_kernel(q_ref, k_ref, v_ref, kc_ref, vc_ref, cos_ref, sin_ref, bias_ref, o_ref,
               qr_ref, kr_ref, vb_ref, octx_ref, mctx_ref, lctx_ref):
    scale = DH_B ** -0.5
    rows = DEC_SEQ // GRID_W
    kh = MAX_KH
    cos = cos_ref[...]
    sin = sin_ref[...]
    q = q_ref[...]
    k = k_ref[...]
    qr_ref[...] = (q * cos + pltpu.roll(q, DH_B // 2, 1) * sin).astype(BF16)
    kr_ref[...] = (k * cos + pltpu.roll(k, DH_B // 2, 1) * sin).astype(BF16)
    vb_ref[...] = v_ref[...].astype(BF16)
    kc = kc_ref[0, 0].astype(BF16)
    vc = vc_ref[0, 0].astype(BF16)

    def ctx_step(t, carry):
        trow = pl.ds(pl.multiple_of(t * NA_CTX_TILE, NA_CTX_TILE), NA_CTX_TILE)
        s = _dot_nt(q_ref[trow, :].astype(BF16), kc) * scale
        m = jnp.max(s, axis=-1, keepdims=True)
        p = jnp.exp(s - m)
        octx_ref[trow, :] = _dot(p.astype(BF16), vc)
        mctx_ref[trow, :] = jnp.broadcast_to(m, (NA_CTX_TILE, LANE))
        lctx_ref[trow, :] = jnp.broadcast_to(jnp.sum(p, axis=-1, keepdims=True), (NA_CTX_TILE, LANE))
        return carry

    lax.fori_loop(0, DEC_SEQ // NA_CTX_TILE, ctx_step, 0)

    n_groups = rows // NA_GROUP
    nq = NA_GROUP * GRID_W
    nk = NA_UNION * GRID_W

    def group_step(g, carry):
        u0 = jnp.clip(g * NA_GROUP - kh // 2, 0, rows - NA_UNION)
        kind = jnp.where(g == 0, 0, jnp.where(g == n_groups - 1, 2, 1))
        qrow = pl.ds(pl.multiple_of(g * nq, nq), nq)
        krow = pl.ds(pl.multiple_of(u0 * GRID_W, GRID_W), nk)
        bias = jnp.concatenate([bias_ref[kind * NA_GROUP + j, 0] for j in range(NA_GROUP)], axis=0)
        s_loc = _dot_nt(qr_ref[qrow, :], kr_ref[krow, :]) * scale + bias
        m_ctx = mctx_ref[qrow, :][:, 0:1]
        m = jnp.maximum(jnp.max(s_loc, axis=-1, keepdims=True), m_ctx)
        p_loc = jnp.exp(s_loc - m)
        w_ctx = jnp.exp(m_ctx - m)
        den = jnp.sum(p_loc, axis=-1, keepdims=True) + lctx_ref[qrow, :][:, 0:1] * w_ctx
        o = _dot(p_loc.astype(BF16), vb_ref[krow, :]) + octx_ref[qrow, :] * w_ctx
        o_ref[qrow, :] = o / den
        return carry

    lax.fori_loop(0, n_groups, group_step, 0)


def _neighbourhood_attention(proj, cache_k, cache_v, cos_t, sin_t, bias_tab, layer):
    ck = cache_k.reshape(DEC_BATCH, DEPTH, PAST_LEN, H_B * DH_B)
    cv = cache_v.reshape(DEC_BATCH, DEPTH, PAST_LEN, H_B * DH_B)
    lat = T_CTX // DEC_SEQ
    return pl.pallas_call(
        _na_kernel,
        grid=(DEC_BATCH, H_B),
        in_specs=[
            pl.BlockSpec((DEC_SEQ, DH_B), lambda b, h: (lat + b, CB_Q + h)),
            pl.BlockSpec((DEC_SEQ, DH_B), lambda b, h: (lat + b, CB_K + h)),
            pl.BlockSpec((DEC_SEQ, DH_B), lambda b, h: (lat + b, CB_V + h)),
            pl.BlockSpec((1, 1, PAST_LEN, DH_B), lambda b, h: (b, layer, 0, h)),
            pl.BlockSpec((1, 1, PAST_LEN, DH_B), lambda b, h: (b, layer, 0, h)),
            pl.BlockSpec((DEC_SEQ, DH_B), lambda b, h: (0, 0)),
            pl.BlockSpec((DEC_SEQ, DH_B), lambda b, h: (0, 0)),
            pl.BlockSpec((3 * NA_GROUP, 1, GRID_W, NA_UNION * GRID_W), lambda b, h: (0, h, 0, 0)),
        ],
        out_specs=pl.BlockSpec((DEC_SEQ, DH_B), lambda b, h: (b, h)),
        out_shape=jax.ShapeDtypeStruct((T_LAT, BRANCH_W), F32),
        scratch_shapes=[pltpu.VMEM((DEC_SEQ, DH_B), BF16) for _ in range(3)]
        + [pltpu.VMEM((DEC_SEQ, DH_B), F32) for _ in range(3)],
        compiler_params=_params(("arbitrary", "arbitrary")),
        name="neighbourhood_attention",
    )(proj, proj, proj, ck, cv, cos_t, sin_t, bias_tab)


def _ref_rows(a, blk, r):
    pieces = [jnp.broadcast_to(a[b0 + r:b0 + r + 1, :], (blk, a.shape[1])) for b0 in range(0, HG_CHUNK, blk)]
    return pieces[0] if len(pieces) == 1 else jnp.concatenate(pieces, axis=0)


def _split3(x):
    hi = x.astype(BF16)
    r1 = x - hi.astype(F32)
    mid = r1.astype(BF16)
    lo = (r1 - mid.astype(F32)).astype(BF16)
    return hi, mid, lo


def _hgrn_chunk(q, z, v, lb, tri, masks, st, backward):
    c = HG_CHUNK
    f = jnp.maximum(lb + (1.0 - lb) * _sigmoid(z), F_MIN)
    gl = jnp.log(f)
    kk = 1.0 - f
    hi, mid, lo = _split3(gl)
    a = _dot(tri, hi) + _dot(tri, mid) + _dot(tri, lo)

    d0 = a - _ref_rows(a, HG_LEVELS[0], HG_LEVELS[0] // 2 if backward else HG_LEVELS[0] // 2 - 1)
    qw = (q * jnp.exp(d0)).astype(BF16)
    kw = (kk * jnp.exp(-d0)).astype(BF16)
    scores = _dot_nt(qw, kw) * masks[0]
    for li, blk in enumerate(HG_LEVELS[1:], start=1):
        dl = a - _ref_rows(a, blk, blk // 2 if backward else blk // 2 - 1)
        w = jnp.exp(-jnp.abs(dl))
        scores = scores + _dot_nt((q * w).astype(BF16), (kk * w).astype(BF16)) * masks[li]
    vb = v.astype(BF16)
    o = _dot(scores.astype(BF16), vb)
    o = o + _dot_nt((q * jnp.exp(a)).astype(BF16), st.astype(BF16))
    a_tot = a[0:1, :] if backward else a[c - 1:c, :]
    kd = (kk * jnp.exp(a_tot - a)).astype(BF16)
    st_new = st * jnp.exp(a_tot) + _dot_tn(vb, kd)
    return o, st_new


def _hgrn_kernel(*refs, seq_len, zero_init, emit_state):
    q_ref, zf_ref, zb_ref, v_ref, g_ref, lb_ref, nw_ref, tri_ref, msk_ref = refs[:9]
    pos = 9
    s0_ref = None
    if not zero_init:
        s0_ref = refs[pos]
        pos += 1
    o_ref = refs[pos]
    pos += 1
    sfin_ref = None
    if emit_state:
        sfin_ref = refs[pos]
        pos += 1
    of_ref, ob_ref, stf_ref, stb_ref = refs[pos:pos + 4]
    nchunk = seq_len // HG_CHUNK
    n_lev = len(HG_LEVELS)
    if zero_init:
        stf_ref[...] = jnp.zeros((DK, DK), F32)
        stb_ref[...] = jnp.zeros((DK, DK), F32)
    else:
        stf_ref[...] = s0_ref[0, 0, 0, 0].T
        stb_ref[...] = s0_ref[0, 0, 1, 0].T
    lb_f = lb_ref[0, 0, 0]
    lb_b = lb_ref[0, 1, 0]

    def body(ci, carry):
        rf = pl.ds(pl.multiple_of(ci * HG_CHUNK, HG_CHUNK), HG_CHUNK)
        rb = pl.ds(pl.multiple_of((nchunk - 1 - ci) * HG_CHUNK, HG_CHUNK), HG_CHUNK)
        o_f, st_f = _hgrn_chunk(q_ref[rf, :], zf_ref[rf, :], v_ref[rf, :], lb_f, tri_ref[0],
                                [msk_ref[0, li] for li in range(n_lev)], stf_ref[...], False)
        o_b, st_b = _hgrn_chunk(q_ref[rb, :], zb_ref[rb, :], v_ref[rb, :], lb_b, tri_ref[1],
                                [msk_ref[1, li] for li in range(n_lev)], stb_ref[...], True)
        stf_ref[...] = st_f
        stb_ref[...] = st_b
        of_ref[rf, :] = o_f
        ob_ref[rb, :] = o_b
        return carry

    lax.fori_loop(0, nchunk, body, 0)

    nw = nw_ref[0]

    def finish(t, carry):
        rows = pl.ds(pl.multiple_of(t * HG_FIN_TILE, HG_FIN_TILE), HG_FIN_TILE)
        o = of_ref[rows, :] + ob_ref[rows, :]
        o = o * lax.rsqrt(jnp.mean(o * o, axis=-1, keepdims=True) + LN_EPS) * nw
        o_ref[rows, :] = o * _silu(g_ref[rows, :])
        return carry

    lax.fori_loop(0, seq_len // HG_FIN_TILE, finish, 0)
    if emit_state:
        sfin_ref[0, 0, 0] = stf_ref[...].T
        sfin_ref[0, 1, 0] = stb_ref[...].T


def _hgrn_consts():
    c = HG_CHUNK
    t = jnp.arange(c)[:, None]
    s = jnp.arange(c)[None, :]
    tril = (s <= t)
    tri = jnp.stack([tril, tril.T]).astype(BF16)
    fwd = [(t // HG_LEVELS[0] == s // HG_LEVELS[0]) & (s <= t)]
    for blk in HG_LEVELS[1:]:
        fwd.append((t // blk == s // blk) & (t % blk >= blk // 2) & (s % blk < blk // 2))
    fwd = jnp.stack(fwd)
    masks = jnp.stack([fwd, jnp.swapaxes(fwd, 1, 2)]).astype(F32)
    return tri, masks


def _hgrn_mixer(proj, lower, norm_w, tri, masks, layer, *, n_seq, seq_len, row_block0, s0):
    zero_init = s0 is None
    seq_spec = lambda cb: pl.BlockSpec((seq_len, DK), lambda b, h: (row_block0 + b, cb + h))
    in_specs = [
        seq_spec(CB_HQ), seq_spec(CB_HFF), seq_spec(CB_HFB), seq_spec(CB_HI), seq_spec(CB_HG),
        pl.BlockSpec((1, 2, 1, 1, DK), lambda b, h: (layer, 0, h, 0, 0)),
        pl.BlockSpec((1, 1, DK), lambda b, h: (layer, 0, 0)),
        pl.BlockSpec((2, HG_CHUNK, HG_CHUNK), lambda b, h: (0, 0, 0)),
        pl.BlockSpec((2, len(HG_LEVELS), HG_CHUNK, HG_CHUNK), lambda b, h: (0, 0, 0, 0)),
    ]
    args = [proj, proj, proj, proj, proj, lower, norm_w, tri, masks]
    if not zero_init:
        in_specs.append(pl.BlockSpec((1, 1, 2, 1, DK, DK), lambda b, h: (b, layer, 0, h, 0, 0)))
        args.append(s0)
    out_specs = [pl.BlockSpec((seq_len, DK), lambda b, h: (b, h))]
    out_shape = [jax.ShapeDtypeStruct((n_seq * seq_len, BRANCH_W), F32)]
    if zero_init:
        out_specs.append(pl.BlockSpec((1, 2, 1, DK, DK), lambda b, h: (b, 0, h, 0, 0)))
        out_shape.append(jax.ShapeDtypeStruct((n_seq, 2, H_C, DK, DK), F32))
    res = pl.pallas_call(
        functools.partial(_hgrn_kernel, seq_len=seq_len, zero_init=zero_init, emit_state=zero_init),
        grid=(n_seq, H_C),
        in_specs=in_specs,
        out_specs=out_specs,
        out_shape=out_shape,
        scratch_shapes=[pltpu.VMEM((seq_len, DK), F32), pltpu.VMEM((seq_len, DK), F32),
                        pltpu.VMEM((DK, DK), F32), pltpu.VMEM((DK, DK), F32)],
        compiler_params=_params(("arbitrary", "arbitrary")),
        name="hgrn2_ctx" if zero_init else "hgrn2_latent",
    )(*args)
    return res if zero_init else (res[0], None)


def _merge_kernel(ya_ref, ybc_ref, ybl_ref, ycc_ref, ycl_ref, ga_ref, gb_ref, gc_ref, w_ref, o_ref, *, n_ctx_tiles):
    is_ctx = pl.program_id(1) < n_ctx_tiles
    yb = jnp.where(is_ctx, ybc_ref[...], ybl_ref[...])
    yc = jnp.where(is_ctx, ycc_ref[...], ycl_ref[...])
    acc = _sigmoid(ga_ref[...]) * _dot(ya_ref[...].astype(BF16), w_ref[0, 0])
    acc = acc + _sigmoid(gb_ref[...]) * _dot(yb.astype(BF16), w_ref[0, 1])
    acc = acc + _sigmoid(gc_ref[...]) * _dot(yc.astype(BF16), w_ref[0, 2])
    o_ref[...] = acc.astype(BF16)


def _merge_branches(ya, yb_ctx, yb_lat, yc_ctx, yc_lat, proj, wb_bf, layer):
    tm, tn = 256, 1024
    nj = D_MODEL // tn
    nc = T_CTX // tm
    gate0 = CB_GATES * LANE // tn
    y_spec = pl.BlockSpec((tm, BRANCH_W), lambda j, i: (i, 0))
    ctx_spec = pl.BlockSpec((tm, BRANCH_W), lambda j, i: (jnp.minimum(i, nc - 1), 0))
    lat_spec = pl.BlockSpec((tm, BRANCH_W), lambda j, i: (jnp.maximum(i - nc, 0), 0))
    gate_spec = lambda k: pl.BlockSpec((tm, tn), lambda j, i: (i, gate0 + k * nj + j))
    return pl.pallas_call(
        functools.partial(_merge_kernel, n_ctx_tiles=nc),
        grid=(nj, T_ALL // tm),
        in_specs=[y_spec, ctx_spec, lat_spec, ctx_spec, lat_spec, gate_spec(0), gate_spec(1), gate_spec(2),
                  pl.BlockSpec((1, 3, BRANCH_W, tn), lambda j, i: (layer, 0, 0, j))],
        out_specs=pl.BlockSpec((tm, tn), lambda j, i: (i, j)),
        out_shape=jax.ShapeDtypeStruct((T_ALL, D_MODEL), BF16),
        compiler_params=_params(("arbitrary", "arbitrary")),
        name="merge_branches",
    )(ya, yb_ctx, yb_lat, yc_ctx, yc_lat, proj, proj, proj, wb_bf)


def _out_router_kernel(mg_ref, x_ref, mod_ref, wo_ref, lnw_ref, lnb_ref, rwh_ref, rwl_ref, rb_ref,
                       x1_ref, h2_ref, rt_ref, *, alpha):
    mix = _dot(mg_ref[...], wo_ref[0])
    g1 = mod_ref[0, 0, 2:3, :]
    x1 = _layer_norm_rows(alpha * x_ref[...] + g1 * mix, lnw_ref[0], lnb_ref[0])
    x1_ref[...] = x1
    h2 = x1 * (1.0 + mod_ref[0, 0, 4:5, :]) + mod_ref[0, 0, 3:4, :]
    h2_ref[...] = h2

    hh = h2.astype(BF16)
    hl = (h2 - hh.astype(F32)).astype(BF16)
    lg = _dot(hh, rwh_ref[0]) + _dot(hl, rwh_ref[0]) + _dot(hh, rwl_ref[0]) + rb_ref[0]

    lane = lax.broadcasted_iota(jnp.int32, lg.shape, 1)
    neg = jnp.float32(-jnp.inf)
    big = jnp.int32(1 << 20)
    is_c = lane < N_GROUPS
    cmax = jnp.max(jnp.where(is_c, lg, neg), axis=-1, keepdims=True)
    g_idx = jnp.min(jnp.where(is_c & (lg == cmax), lane, big), axis=-1, keepdims=True)
    p_group = 1.0 / jnp.sum(jnp.where(is_c, jnp.exp(lg - cmax), 0.0), axis=-1, keepdims=True)
    lo_lane = N_GROUPS + EXPERTS_PER_GROUP * g_idx
    in_g = (lane >= lo_lane) & (lane < lo_lane + EXPERTS_PER_GROUP)
    v1 = jnp.max(jnp.where(in_g, lg, neg), axis=-1, keepdims=True)
    i1 = jnp.min(jnp.where(in_g & (lg == v1), lane, big), axis=-1, keepdims=True)
    rest = in_g & (lane != i1)
    v2 = jnp.max(jnp.where(rest, lg, neg), axis=-1, keepdims=True)
    i2 = jnp.min(jnp.where(rest & (lg == v2), lane, big), axis=-1, keepdims=True)
    e21 = jnp.exp(v2 - v1)
    w1 = p_group / (1.0 + e21)
    w2 = p_group * e21 / (1.0 + e21)
    e1 = (i1 - N_GROUPS).astype(F32)
    e2 = (i2 - N_GROUPS).astype(F32)
    rt_ref[...] = jnp.where(lane == 0, e1, jnp.where(lane == 1, e2, jnp.where(lane == 2, w1, jnp.where(lane == 3, w2, 0.0))))


def _out_proj_router(merged, x, mods, wo_bf, ln_w, ln_b, rw_hi, rw_lo, rb, layer, alpha):
    tm = 256
    row = pl.BlockSpec((tm, D_MODEL), lambda i: (i, 0))
    vec = pl.BlockSpec((1, 1, D_MODEL), lambda i: (layer, 0, 0))
    rw_spec = pl.BlockSpec((1, D_MODEL, ROUTE_LANES), lambda i: (layer, 0, 0))
    return pl.pallas_call(
        functools.partial(_out_router_kernel, alpha=alpha),
        grid=(T_ALL // tm,),
        in_specs=[
            row, row,
            pl.BlockSpec((1, 1, 6, D_MODEL), lambda i: (layer, (i * tm) // SLAB, 0, 0)),
            pl.BlockSpec((1, D_MODEL, D_MODEL), lambda i: (layer, 0, 0)),
            vec, vec, rw_spec, rw_spec,
            pl.BlockSpec((1, 1, ROUTE_LANES), lambda i: (layer, 0, 0)),
        ],
        out_specs=[row, row, pl.BlockSpec((tm, ROUTE_LANES), lambda i: (i, 0))],
        out_shape=[jax.ShapeDtypeStruct((T_ALL, D_MODEL), F32), jax.ShapeDtypeStruct((T_ALL, D_MODEL), F32),
                   jax.ShapeDtypeStruct((T_ALL, ROUTE_LANES), F32)],
        compiler_params=_params(("arbitrary",)),
        name="out_proj_router",
    )(merged, x, mods, wo_bf, ln_w, ln_b, rw_hi, rw_lo, rb)


def _row_gather(src_hbm, idx_of, dst_ref, sem, n_rows):
    def issue(r, carry):
        pltpu.make_async_copy(src_hbm.at[pl.ds(idx_of(r), 1)], dst_ref.at[pl.ds(r, 1)], sem).start()
        return carry
    lax.fori_loop(0, n_rows, issue, 0, unroll=ROW_ISSUE_UNROLL)


def _row_gather_wait(src_hbm, dst_ref, sem, n_rows):
    pltpu.make_async_copy(src_hbm.at[pl.ds(0, n_rows)], dst_ref, sem).wait()


def _moe_kernel(be_ref, nused_ref, tok_ref, tokn_ref, h2_hbm, w1_ref, w3_ref, w2_ref, rw_ref, o_ref, xbuf, sem):
    del be_ref
    i = pl.program_id(0)
    n_used = nused_ref[0]
    slot = lax.rem(i, 2)

    @pl.when((i == 0) & (n_used > 0))
    def _():
        _row_gather(h2_hbm, lambda r: tok_ref[0, 0, r], xbuf.at[0], sem.at[0], MOE_BLK)

    @pl.when(i + 1 < n_used)
    def _():
        _row_gather(h2_hbm, lambda r: tokn_ref[0, 0, r], xbuf.at[1 - slot], sem.at[1 - slot], MOE_BLK)

    @pl.when(i < n_used)
    def _():
        _row_gather_wait(h2_hbm, xbuf.at[slot], sem.at[slot], MOE_BLK)
        xb = xbuf[slot].astype(BF16)
        hid = _silu(_dot(xb, w1_ref[0, 0])) * _dot(xb, w3_ref[0, 0])
        o_ref[...] = _dot(hid.astype(BF16), w2_ref[0, 0]) * rw_ref[...]

    @pl.when(i >= n_used)
    def _():
        o_ref[...] = jnp.zeros(o_ref.shape, F32)


def _routed_experts(h2, block_expert, n_used, row_tok, row_w, w1_bf, w3_bf, w2_bf, layer):
    tok3 = row_tok.reshape(N_MOE_BLOCKS, 1, MOE_BLK)
    grid_spec = pltpu.PrefetchScalarGridSpec(
        num_scalar_prefetch=2,
        grid=(N_MOE_BLOCKS,),
        in_specs=[
            pl.BlockSpec((1, 1, MOE_BLK), lambda i, be, nu: (i, 0, 0), memory_space=pltpu.SMEM),
            pl.BlockSpec((1, 1, MOE_BLK), lambda i, be, nu: (jnp.minimum(i + 1, N_MOE_BLOCKS - 1), 0, 0),
                         memory_space=pltpu.SMEM),
            pl.BlockSpec(memory_space=pl.ANY),
            pl.BlockSpec((1, 1, D_MODEL, D_EXPERT), lambda i, be, nu: (layer, be[i], 0, 0)),
            pl.BlockSpec((1, 1, D_MODEL, D_EXPERT), lambda i, be, nu: (layer, be[i], 0, 0)),
            pl.BlockSpec((1, 1, D_EXPERT, D_MODEL), lambda i, be, nu: (layer, be[i], 0, 0)),
            pl.BlockSpec((MOE_BLK, 1), lambda i, be, nu: (i, 0)),
        ],
        out_specs=pl.BlockSpec((MOE_BLK, D_MODEL), lambda i, be, nu: (i, 0)),
        scratch_shapes=[pltpu.VMEM((2, MOE_BLK, D_MODEL), F32), pltpu.SemaphoreType.DMA((2,))],
    )
    return pl.pallas_call(
        _moe_kernel,
        grid_spec=grid_spec,
        out_shape=jax.ShapeDtypeStruct((N_MOE_ROWS, D_MODEL), F32),
        compiler_params=_params(("arbitrary",)),
        name="routed_experts",
    )(block_expert, n_used, tok3, tok3, h2, w1_bf, w3_bf, w2_bf, row_w.reshape(N_MOE_ROWS, 1))


def _combine_kernel(pos_ref, posn_ref, x1_ref, mod_ref, lnw_ref, lnb_ref, yb_hbm, o_ref, buf, sem, *, alpha, tm):
    i = pl.program_id(0)
    slot = lax.rem(i, 2)

    def start(p_ref, s):
        for k in range(TOP_K):
            _row_gather(yb_hbm, lambda r, k=k: p_ref[0, k, r], buf.at[TOP_K * s + k], sem.at[TOP_K * s + k], tm)

    @pl.when(i == 0)
    def _():
        start(pos_ref, 0)

    @pl.when(i + 1 < pl.num_programs(0))
    def _():
        start(posn_ref, 1 - slot)

    for k in range(TOP_K):
        _row_gather_wait(yb_hbm, buf.at[TOP_K * slot + k], sem.at[TOP_K * slot + k], tm)
    g2 = mod_ref[0, 0, 5:6, :]
    y = alpha * x1_ref[...] + g2 * (buf[TOP_K * slot] + buf[TOP_K * slot + 1])
    o_ref[...] = _layer_norm_rows(y, lnw_ref[0], lnb_ref[0])


def _combine_norm(pos, x1, mods, ln_w, ln_b, yb, layer, alpha):
    tm = COMBINE_TILE
    n_tiles = T_ALL // tm
    row = pl.BlockSpec((tm, D_MODEL), lambda i: (i, 0))
    vec = pl.BlockSpec((1, 1, D_MODEL), lambda i: (layer, 0, 0))
    return pl.pallas_call(
        functools.partial(_combine_kernel, alpha=alpha, tm=tm),
        grid=(n_tiles,),
        in_specs=[
            pl.BlockSpec((1, TOP_K, tm), lambda i: (i, 0, 0), memory_space=pltpu.SMEM),
            pl.BlockSpec((1, TOP_K, tm), lambda i: (jnp.minimum(i + 1, n_tiles - 1), 0, 0), memory_space=pltpu.SMEM),
            row,
            pl.BlockSpec((1, 1, 6, D_MODEL), lambda i: (layer, (i * tm) // SLAB, 0, 0)),
            vec, vec,
            pl.BlockSpec(memory_space=pl.ANY),
        ],
        out_specs=row,
        out_shape=jax.ShapeDtypeStruct((T_ALL, D_MODEL), F32),
        scratch_shapes=[pltpu.VMEM((2 * TOP_K, tm, D_MODEL), F32), pltpu.SemaphoreType.DMA((2 * TOP_K,))],
        compiler_params=_params(("arbitrary",)),
        name="combine_norm",
    )(pos, pos, x1, mods, ln_w, ln_b, yb)


def _dispatch_plan(route):
    ids = route[:, 0:TOP_K].astype(jnp.int32)
    wts = route[:, TOP_K:2 * TOP_K]
    flat_e = ids.reshape(-1)
    flat_w = wts.reshape(-1)
    onehot = (flat_e[:, None] == jnp.arange(N_EXPERTS, dtype=jnp.int32)[None, :]).astype(jnp.int32)
    grp = 256
    oh3 = onehot.reshape(-1, grp, N_EXPERTS)
    tril = (jnp.arange(grp)[None, :] <= jnp.arange(grp)[:, None]).astype(F32)
    inner = jnp.einsum('ts,gse->gte', tril, oh3.astype(F32), precision=lax.Precision.HIGHEST).astype(jnp.int32)
    gsum = jnp.sum(oh3, axis=1)
    goff = jnp.cumsum(gsum, axis=0) - gsum
    csum = (inner + goff[:, None, :]).reshape(-1, N_EXPERTS)
    rank = jnp.sum(csum * onehot, axis=1) - 1
    counts = jnp.sum(gsum, axis=0)
    padded = (counts + MOE_BLK - 1) // MOE_BLK * MOE_BLK
    pad_end = jnp.cumsum(padded)
    pad_start = pad_end - padded
    dest = (jnp.sum(onehot * pad_start[None, :], axis=1) + rank).astype(jnp.int32)
    tok = jnp.arange(T_ALL * TOP_K, dtype=jnp.int32) // TOP_K
    fields = jnp.stack([tok, lax.bitcast_convert_type(flat_w, jnp.int32)], axis=1)
    rows = jnp.zeros((N_MOE_ROWS, 2), jnp.int32).at[dest].set(fields)
    row_tok = rows[:, 0]
    row_w = lax.bitcast_convert_type(rows[:, 1], F32)
    blk_row0 = jnp.arange(N_MOE_BLOCKS, dtype=jnp.int32) * MOE_BLK
    block_expert = jnp.clip(jnp.sum((pad_end[None, :] <= blk_row0[:, None]).astype(jnp.int32), axis=1),
                            0, N_EXPERTS - 1).astype(jnp.int32)
    n_used = (pad_end[-1] // MOE_BLK).astype(jnp.int32).reshape(1)
    tm = COMBINE_TILE
    pos = dest.reshape(T_ALL // tm, tm, TOP_K).transpose(0, 2, 1)
    return block_expert, n_used, row_tok, row_w, pos


def _rope_tables():
    pos = jnp.arange(DEC_SEQ)
    row = (pos // GRID_W).astype(F32)
    col = (pos % GRID_W).astype(F32)
    n_freq = DH_B // 4
    inv = ROPE_BASE ** (-jnp.arange(n_freq, dtype=F32) / n_freq)
    ang = jnp.concatenate([row[:, None] * inv, col[:, None] * inv], axis=-1)
    cos, sin = jnp.cos(ang), jnp.sin(ang)
    return jnp.concatenate([cos, cos], axis=-1), jnp.concatenate([-sin, sin], axis=-1)


def _na_bias_table(rpb):
    cols = jnp.arange(GRID_W)
    col_start = jnp.clip(cols - KW // 2, 0, GRID_W - KW)
    col_mask = (cols[None, :] >= col_start[:, None]) & (cols[None, :] < col_start[:, None] + KW)
    dx_idx = jnp.clip(cols[None, :] - cols[:, None] + KW - 1, 0, 2 * KW - 2)
    onehot = (dx_idx[:, :, None] == jnp.arange(2 * KW - 1)[None, None, :]).astype(F32)
    toep = jnp.einsum('qkx,hyx->hyqk', onehot, rpb, precision=lax.Precision.HIGHEST)
    toep = jnp.where(col_mask[None, None], toep, MASK_VALUE)
    rows = DEC_SEQ // GRID_W
    n_groups = rows // NA_GROUP
    masked = jnp.full((H_B, GRID_W, GRID_W), MASK_VALUE, F32)
    variants = []
    for g_rep in (0, 1, n_groups - 1):
        u0 = min(max(g_rep * NA_GROUP - MAX_KH // 2, 0), rows - NA_UNION)
        for j in range(NA_GROUP):
            r = g_rep * NA_GROUP + j
            start = min(max(r - MAX_KH // 2, 0), rows - MAX_KH)
            per_row = [toep[:, u0 + i - r + MAX_KH - 1] if start <= u0 + i < start + MAX_KH else masked
                       for i in range(NA_UNION)]
            variants.append(jnp.stack(per_row, axis=2))
    return jnp.stack(variants).reshape(3 * NA_GROUP, H_B, GRID_W, NA_UNION * GRID_W)


def kernel(x_prompt, x_sample, cache_k, cache_v, state_hgrn, c, c_ctx, w_mod, b_mod, w_in, gmlp_ws, gmlp_bs, na_rpb, hgrn_lb, hgrn_norm_w, w_branch, w_o, ln1_w, ln1_b, ln2_w, ln2_b, router_w1, router_b1, router_w2, router_b2, moe_w1, moe_w3, moe_w2):
    alpha = (2.0 * DEPTH) ** 0.25
    x = jnp.concatenate([x_prompt.reshape(T_CTX, D_MODEL), x_sample.reshape(T_LAT, D_MODEL)], axis=0)
    cond = jnp.concatenate([c_ctx[None, :], c, jnp.zeros((MOD_ROWS - 1 - DEC_BATCH, D_MODEL), F32)], axis=0)
    mods = _modulation(cond, w_mod, b_mod)

    p_lb = jax.nn.softmax(hgrn_lb.astype(F32), axis=0)
    lower = (jnp.cumsum(p_lb, axis=0) - p_lb[0:1]).reshape(DEPTH, 2, H_C, 1, DK)
    norm_w = hgrn_norm_w.reshape(DEPTH, 1, DK)
    tri, masks = _hgrn_consts()
    cos_t, sin_t = _rope_tables()

    ws_bf = gmlp_ws.astype(BF16)
    bs_b = jnp.repeat(jnp.swapaxes(gmlp_bs, 1, 2), LANE, axis=2)
    wb_bf = w_branch.astype(BF16)
    wo_bf = w_o.astype(BF16)
    rw = jnp.concatenate([router_w1, router_w2.transpose(0, 2, 1, 3).reshape(DEPTH, D_MODEL, N_EXPERTS),
                          jnp.zeros((DEPTH, D_MODEL, ROUTE_LANES - N_GROUPS - N_EXPERTS), F32)], axis=-1)
    rw_hi = rw.astype(BF16)
    rw_lo = (rw - rw_hi.astype(F32)).astype(BF16)
    rb = jnp.concatenate([router_b1, router_b2.reshape(DEPTH, N_EXPERTS),
                          jnp.zeros((DEPTH, ROUTE_LANES - N_GROUPS - N_EXPERTS), F32)], axis=-1).reshape(DEPTH, 1, ROUTE_LANES)
    w1_bf = moe_w1.astype(BF16)
    w3_bf = moe_w3.astype(BF16)
    w2_bf = moe_w2.astype(BF16)
    ln1w, ln1b = ln1_w.reshape(DEPTH, 1, D_MODEL), ln1_b.reshape(DEPTH, 1, D_MODEL)
    ln2w, ln2b = ln2_w.reshape(DEPTH, 1, D_MODEL), ln2_b.reshape(DEPTH, 1, D_MODEL)

    ks_out, vs_out, ss_out = [], [], []
    for l in range(DEPTH):
        proj = _in_projection(x, mods, w_in, l)
        ya = _gmlp_mixer(proj, ws_bf, bs_b, l)
        yb_ctx = _context_attention(proj)
        yb_lat = _neighbourhood_attention(proj, cache_k, cache_v, cos_t, sin_t, _na_bias_table(na_rpb[l]), l)
        yc_ctx, s_ctx = _hgrn_mixer(proj, lower, norm_w, tri, masks, l,
                                    n_seq=BATCH, seq_len=SEQ, row_block0=0, s0=None)
        yc_lat, _ = _hgrn_mixer(proj, lower, norm_w, tri, masks, l,
                                n_seq=DEC_BATCH, seq_len=DEC_SEQ, row_block0=T_CTX // DEC_SEQ, s0=state_hgrn)
        merged = _merge_branches(ya, yb_ctx, yb_lat, yc_ctx, yc_lat, proj, wb_bf, l)
        x1, h2, route = _out_proj_router(merged, x, mods, wo_bf, ln1w, ln1b, rw_hi, rw_lo, rb, l, alpha)
        block_expert, n_used, row_tok, row_w, pos = _dispatch_plan(route)
        ybk = _routed_experts(h2, block_expert, n_used, row_tok, row_w, w1_bf, w3_bf, w2_bf, l)
        x = _combine_norm(pos, x1, mods, ln2w, ln2b, ybk, l, alpha)

        kv = proj[:T_CTX, CB_K * LANE:(CB_V + H_B) * LANE]
        ks_out.append(kv[:, :BRANCH_W].reshape(BATCH, SEQ, H_B, DH_B))
        vs_out.append(kv[:, BRANCH_W:].reshape(BATCH, SEQ, H_B, DH_B))
        ss_out.append(s_ctx)

    y_prompt = x[:T_CTX].reshape(BATCH, SEQ, D_MODEL)
    y_sample = x[T_CTX:].reshape(DEC_BATCH, DEC_SEQ, D_MODEL)
    return (y_prompt, y_sample, jnp.stack(ks_out, axis=1), jnp.stack(vs_out, axis=1), jnp.stack(ss_out, axis=1))
```

```python
import functools

import numpy as np
import jax
import jax.numpy as jnp
from jax import lax
from jax.experimental import pallas as pl
from jax.experimental.pallas import tpu as pltpu

F32 = jnp.float32
BF16 = jnp.bfloat16

D_MODEL = 2048
BATCH = 16
SEQ = 256
DEPTH = 2
DEC_BATCH = 4
DEC_SEQ = 4096
PAST_LEN = 512
GRID_W = 64
BRANCH_W = D_MODEL // 2
CHUNK_A = 128
G_A = 8
DH_B = 128
H_B = 8
MAX_KH = 8
KW = 16
ROPE_BASE = 10000.0
DK = 128
H_C = 8
N_GROUPS = 4
EXPERTS_PER_GROUP = 8
N_EXPERTS = 32
TOP_K = 2
D_EXPERT = D_MODEL // 2
LN_EPS = 1e-5
MASK_VALUE = -1e30
F_MIN = 1e-6
IN_COLS = 16384

T_CTX = BATCH * SEQ
T_LAT = DEC_BATCH * DEC_SEQ
T_ALL = T_CTX + T_LAT
SLAB = 4096
MOD_ROWS = 8
LANE = 128

PROJ_TN = 2048
PB_TILE = 3
PA_COLS = IN_COLS - PROJ_TN
CB_Q, CB_K, CB_V = 16, 24, 32
CB_HQ, CB_HI, CB_HG = 40, 48, 56
CB_GATES = 64
CB_HFF, CB_HFB = 0, 8

HG_CHUNK = 128
HG_LEVELS = (8, 16, 32, 64, 128)
HG_FIN_TILE = 256
HG_HEADS = 2
NA_CTX_TILE = 512
NA_GROUP = 4
NA_UNION = NA_GROUP + MAX_KH
MOE_BLK = 256
N_MOE_ROWS = T_ALL * TOP_K + N_EXPERTS * MOE_BLK
N_MOE_BLOCKS = N_MOE_ROWS // MOE_BLK
ROUTE_LANES = 128
COMBINE_TILE = 256
ROW_ISSUE_UNROLL = 8

VMEM_LIMIT = 56 * 1024 * 1024


def _params(sem):
    return pltpu.CompilerParams(dimension_semantics=sem, vmem_limit_bytes=VMEM_LIMIT)


def _sigmoid(x):
    return 1.0 / (1.0 + jnp.exp(-x))


def _silu(x):
    return x * _sigmoid(x)


def _gelu_tanh(x):
    return x * (0.5 * (1.0 + jnp.tanh(0.7978845608028654 * (x + 0.044715 * (x * x * x)))))


def _layer_norm_rows(y, w, b):
    mu = jnp.mean(y, axis=-1, keepdims=True)
    yc = y - mu
    var = jnp.mean(yc * yc, axis=-1, keepdims=True)
    return yc * lax.rsqrt(var + LN_EPS) * w + b


def _dot(a, b):
    return jnp.dot(a, b, preferred_element_type=F32)


def _dot_nt(a, b):
    return lax.dot_general(a, b, (((1,), (1,)), ((), ())), preferred_element_type=F32)


def _dot_tn(a, b):
    return lax.dot_general(a, b, (((0,), (0,)), ((), ())), preferred_element_type=F32)


def _mod_kernel(c_ref, w_ref, b_ref, o_ref):
    o_ref[0] = _dot(_silu(c_ref[...]), w_ref[0]) + b_ref[0]


def _modulation(cond, w_mod, b_mod):
    tn = 1024
    n = 6 * D_MODEL
    out = pl.pallas_call(
        _mod_kernel,
        grid=(DEPTH, n // tn),
        in_specs=[
            pl.BlockSpec((MOD_ROWS, D_MODEL), lambda l, j: (0, 0)),
            pl.BlockSpec((1, D_MODEL, tn), lambda l, j: (l, 0, j)),
            pl.BlockSpec((1, 1, tn), lambda l, j: (l, 0, j)),
        ],
        out_specs=pl.BlockSpec((1, MOD_ROWS, tn), lambda l, j: (l, 0, j)),
        out_shape=jax.ShapeDtypeStruct((DEPTH, MOD_ROWS, n), F32),
        compiler_params=_params(("arbitrary", "arbitrary")),
        name="modulation",
    )(cond, w_mod, b_mod.reshape(DEPTH, 1, n))
    return out.reshape(DEPTH, MOD_ROWS, 6, D_MODEL)


def _mod_spec(layer, tm, grid_rank=1):
    if grid_rank == 1:
        return pl.BlockSpec((1, 1, 6, D_MODEL), lambda i: (layer, (i * tm) // SLAB, 0, 0))
    return pl.BlockSpec((1, 1, 6, D_MODEL), lambda j, i: (layer, (i * tm) // SLAB, 0, 0))


def _stack_kernel(xp_ref, xs_ref, mod_ref, x_ref, h_ref, *, n_ctx_tiles):
    x = jnp.where(pl.program_id(0) < n_ctx_tiles, xp_ref[...], xs_ref[...])
    x_ref[...] = x
    h_ref[...] = (x * (1.0 + mod_ref[0, 0, 1:2, :]) + mod_ref[0, 0, 0:1, :]).astype(BF16)


def _stack_modulate(x_prompt, x_sample, mods):
    tm = 512
    nc = T_CTX // tm
    row = pl.BlockSpec((tm, D_MODEL), lambda i: (i, 0))
    return pl.pallas_call(
        functools.partial(_stack_kernel, n_ctx_tiles=nc),
        grid=(T_ALL // tm,),
        in_specs=[
            pl.BlockSpec((tm, D_MODEL), lambda i: (jnp.minimum(i, nc - 1), 0)),
            pl.BlockSpec((tm, D_MODEL), lambda i: (jnp.maximum(i - nc, 0), 0)),
            _mod_spec(0, tm),
        ],
        out_specs=[row, row],
        out_shape=[jax.ShapeDtypeStruct((T_ALL, D_MODEL), F32), jax.ShapeDtypeStruct((T_ALL, D_MODEL), BF16)],
        compiler_params=_params(("arbitrary",)),
        name="stack_modulate",
    )(x_prompt.reshape(T_CTX, D_MODEL), x_sample.reshape(T_LAT, D_MODEL), mods)


def _inproj_kernel(h_ref, w_ref, o_ref):
    o_ref[...] = _dot(h_ref[...], w_ref[0]).astype(o_ref.dtype)


def _in_projection(h, w_in_bf, layer):
    tm, tn = 1024, PROJ_TN
    n_tiles = IN_COLS // tn

    def call(col_tile_of, n_out_tiles, dtype, name):
        return pl.pallas_call(
            _inproj_kernel,
            grid=(n_out_tiles, T_ALL // tm),
            in_specs=[
                pl.BlockSpec((tm, D_MODEL), lambda j, i: (i, 0)),
                pl.BlockSpec((1, D_MODEL, tn), lambda j, i: (layer, 0, col_tile_of(j))),
            ],
            out_specs=pl.BlockSpec((tm, tn), lambda j, i: (i, j)),
            out_shape=jax.ShapeDtypeStruct((T_ALL, n_out_tiles * tn), dtype),
            compiler_params=_params(("arbitrary", "arbitrary")),
            name=name,
        )(h, w_in_bf)

    pa = call(lambda j: j + (j >= PB_TILE).astype(jnp.int32), n_tiles - 1, BF16, "in_projection_a")
    pb = call(lambda j: j + PB_TILE, 1, F32, "in_projection_b")
    return pa, pb


def _gmlp_kernel(uv_ref, ws_ref, bs_ref, o_ref, *, tm):
    ge = _gelu_tanh(uv_ref[...].astype(F32))
    u = ge[:, :BRANCH_W]
    v = ge[:, BRANCH_W:]
    mu = jnp.mean(v, axis=-1, keepdims=True)
    vc = v - mu
    var = jnp.mean(vc * vc, axis=-1, keepdims=True)
    vn = (vc * lax.rsqrt(var + LN_EPS)).astype(BF16)
    for n in range(tm // CHUNK_A):
        r = slice(n * CHUNK_A, (n + 1) * CHUNK_A)
        for g in range(G_A):
            c = slice(g * LANE, (g + 1) * LANE)
            sv = _dot(ws_ref[0, g], vn[r, c]) + bs_ref[0, :, c]
            o_ref[r, c] = (u[r, c] * sv).astype(BF16)


def _gmlp_mixer(pa, ws_bf, bs_b, layer):
    tm = 256
    return pl.pallas_call(
        functools.partial(_gmlp_kernel, tm=tm),
        grid=(T_ALL // tm,),
        in_specs=[
            pl.BlockSpec((tm, 2 * BRANCH_W), lambda i: (i, 0)),
            pl.BlockSpec((1, G_A, CHUNK_A, CHUNK_A), lambda i: (layer, 0, 0, 0)),
            pl.BlockSpec((1, CHUNK_A, BRANCH_W), lambda i: (layer, 0, 0)),
        ],
        out_specs=pl.BlockSpec((tm, BRANCH_W), lambda i: (i, 0)),
        out_shape=jax.ShapeDtypeStruct((T_ALL, BRANCH_W), BF16),
        compiler_params=_params(("arbitrary",)),
        name="gmlp_mixer",
    )(pa, ws_bf, bs_b)


def _ctx_attn_kernel(q_ref, k_ref, v_ref, o_ref):
    scale = DH_B ** -0.5
    s = _dot_nt(q_ref[...], k_ref[...]) * scale
    m = jnp.max(s, axis=-1, keepdims=True)
    p = jnp.exp(s - m)
    den = jnp.sum(p, axis=-1, keepdims=True)
    o_ref[...] = (_dot(p.astype(BF16), v_ref[...]) / den).astype(BF16)


def _context_attention(pa):
    return pl.pallas_call(
        _ctx_attn_kernel,
        grid=(BATCH, H_B),
        in_specs=[
            pl.BlockSpec((SEQ, DH_B), lambda b, h: (b, CB_Q + h)),
            pl.BlockSpec((SEQ, DH_B), lambda b, h: (b, CB_K + h)),
            pl.BlockSpec((SEQ, DH_B), lambda b, h: (b, CB_V + h)),
        ],
        out_specs=pl.BlockSpec((SEQ, DH_B), lambda b, h: (b, h)),
        out_shape=jax.ShapeDtypeStruct((T_CTX, BRANCH_W), BF16),
        compiler_params=_params(("arbitrary", "arbitrary")),
        name="context_attention",
    )(pa, pa, pa)


def _na_kernel(q_ref, k_ref, v_ref, kc_ref, vc_ref, cos_ref, sin_ref, bias_ref, o_ref,
               qr_ref, kr_ref, octx_ref, mctx_ref, lctx_ref):
    scale = DH_B ** -0.5
    rows = DEC_SEQ // GRID_W
    kh = MAX_KH
    cos = cos_ref[...]
    sin = sin_ref[...]
    q = q_ref[...].astype(F32)
    k = k_ref[...].astype(F32)
    qr_ref[...] = (q * cos + pltpu.roll(q, DH_B // 2, 1) * sin).astype(BF16)
    kr_ref[...] = (k * cos + pltpu.roll(k, DH_B // 2, 1) * sin).astype(BF16)
    kc = kc_ref[0, 0].astype(BF16)
    vc = vc_ref[0, 0].astype(BF16)

    def ctx_step(t, carry):
        trow = pl.ds(pl.multiple_of(t * NA_CTX_TILE, NA_CTX_TILE), NA_CTX_TILE)
        s = _dot_nt(q_ref[trow, :], kc) * scale
        m = jnp.max(s, axis=-1, keepdims=True)
        p = jnp.exp(s - m)
        octx_ref[trow, :] = _dot(p.astype(BF16), vc)
        mctx_ref[trow, :] = jnp.broadcast_to(m, (NA_CTX_TILE, LANE))
        lctx_ref[trow, :] = jnp.broadcast_to(jnp.sum(p, axis=-1, keepdims=True), (NA_CTX_TILE, LANE))
        return carry

    lax.fori_loop(0, DEC_SEQ // NA_CTX_TILE, ctx_step, 0)

    n_groups = rows // NA_GROUP
    nq = NA_GROUP * GRID_W
    nk = NA_UNION * GRID_W

    def group_step(g, carry):
        u0 = jnp.clip(g * NA_GROUP - kh // 2, 0, rows - NA_UNION)
        kind = jnp.where(g == 0, 0, jnp.where(g == n_groups - 1, 2, 1))
        qrow = pl.ds(pl.multiple_of(g * nq, nq), nq)
        krow = pl.ds(pl.multiple_of(u0 * GRID_W, GRID_W), nk)
        bias = jnp.concatenate([bias_ref[kind * NA_GROUP + j, 0] for j in range(NA_GROUP)], axis=0)
        s_loc = _dot_nt(qr_ref[qrow, :], kr_ref[krow, :]) * scale + bias
        m_ctx = mctx_ref[qrow, :][:, 0:1]
        m = jnp.maximum(jnp.max(s_loc, axis=-1, keepdims=True), m_ctx)
        p_loc = jnp.exp(s_loc - m)
        w_ctx = jnp.exp(m_ctx - m)
        den = jnp.sum(p_loc, axis=-1, keepdims=True) + lctx_ref[qrow, :][:, 0:1] * w_ctx
        o = _dot(p_loc.astype(BF16), v_ref[krow, :]) + octx_ref[qrow, :] * w_ctx
        o_ref[qrow, :] = (o / den).astype(BF16)
        return carry

    lax.fori_loop(0, n_groups, group_step, 0)


def _neighbourhood_attention(pa, cache_k, cache_v, cos_t, sin_t, bias_tab, layer):
    ck = cache_k.reshape(DEC_BATCH, DEPTH, PAST_LEN, H_B * DH_B)
    cv = cache_v.reshape(DEC_BATCH, DEPTH, PAST_LEN, H_B * DH_B)
    lat = T_CTX // DEC_SEQ
    return pl.pallas_call(
        _na_kernel,
        grid=(DEC_BATCH, H_B),
        in_specs=[
            pl.BlockSpec((DEC_SEQ, DH_B), lambda b, h: (lat + b, CB_Q + h)),
            pl.BlockSpec((DEC_SEQ, DH_B), lambda b, h: (lat + b, CB_K + h)),
            pl.BlockSpec((DEC_SEQ, DH_B), lambda b, h: (lat + b, CB_V + h)),
            pl.BlockSpec((1, 1, PAST_LEN, DH_B), lambda b, h: (b, layer, 0, h)),
            pl.BlockSpec((1, 1, PAST_LEN, DH_B), lambda b, h: (b, layer, 0, h)),
            pl.BlockSpec((DEC_SEQ, DH_B), lambda b, h: (0, 0)),
            pl.BlockSpec((DEC_SEQ, DH_B), lambda b, h: (0, 0)),
            pl.BlockSpec((3 * NA_GROUP, 1, GRID_W, NA_UNION * GRID_W), lambda b, h: (0, h, 0, 0)),
        ],
        out_specs=pl.BlockSpec((DEC_SEQ, DH_B), lambda b, h: (b, h)),
        out_shape=jax.ShapeDtypeStruct((T_LAT, BRANCH_W), BF16),
        scratch_shapes=[pltpu.VMEM((DEC_SEQ, DH_B), BF16) for _ in range(2)]
        + [pltpu.VMEM((DEC_SEQ, DH_B), F32) for _ in range(3)],
        compiler_params=_params(("arbitrary", "arbitrary")),
        name="neighbourhood_attention",
    )(pa, pa, pa, ck, cv, cos_t, sin_t, bias_tab)


def _ref_rows(a, blk, r):
    pieces = [jnp.broadcast_to(a[b0 + r:b0 + r + 1, :], (blk, a.shape[1])) for b0 in range(0, HG_CHUNK, blk)]
    return pieces[0] if len(pieces) == 1 else jnp.concatenate(pieces, axis=0)


def _hgrn_chunk(q, z, vb, lb, tri, masks, st, backward):
    c = HG_CHUNK
    f = jnp.maximum(lb + (1.0 - lb) * _sigmoid(z), F_MIN)
    gl = jnp.log(f)
    kk = 1.0 - f
    hi = gl.astype(BF16)
    lo = (gl - hi.astype(F32)).astype(BF16)
    a = _dot(tri, hi) + _dot(tri, lo)

    d0 = a - _ref_rows(a, HG_LEVELS[0], HG_LEVELS[0] // 2 if backward else HG_LEVELS[0] // 2 - 1)
    qw = (q * jnp.exp(d0)).astype(BF16)
    kw = (kk * jnp.exp(-d0)).astype(BF16)
    scores = _dot_nt(qw, kw) * masks[0]
    for li, blk in enumerate(HG_LEVELS[1:], start=1):
        dl = a - _ref_rows(a, blk, blk // 2 if backward else blk // 2 - 1)
        w = jnp.exp(-jnp.abs(dl))
        scores = scores + _dot_nt((q * w).astype(BF16), (kk * w).astype(BF16)) * masks[li]
    o = _dot(scores.astype(BF16), vb)
    o = o + _dot_nt((q * jnp.exp(a)).astype(BF16), st.astype(BF16))
    a_tot = a[0:1, :] if backward else a[c - 1:c, :]
    kd = (kk * jnp.exp(a_tot - a)).astype(BF16)
    st_new = st * jnp.exp(a_tot) + _dot_tn(vb, kd)
    return o, st_new


def _hgrn_kernel(*refs, seq_len, zero_init, emit_state):
    q_ref, v_ref, g_ref, zf_ref, zb_ref, lb_ref, nw_ref, tri_ref, msk_ref = refs[:9]
    pos = 9
    s0_ref = None
    if not zero_init:
        s0_ref = refs[pos]
        pos += 1
    o_ref = refs[pos]
    pos += 1
    sfin_ref = None
    if emit_state:
        sfin_ref = refs[pos]
        pos += 1
    of_ref, ob_ref, st_ref = refs[pos:pos + 3]
    nchunk = seq_len // HG_CHUNK
    n_lev = len(HG_LEVELS)
    heads = range(HG_HEADS)
    hcol = [slice(hh * DK, (hh + 1) * DK) for hh in heads]
    for hh in heads:
        for d in (0, 1):
            if zero_init:
                st_ref[2 * hh + d] = jnp.zeros((DK, DK), F32)
            else:
                st_ref[2 * hh + d] = s0_ref[0, 0, d, hh].T

    def body(ci, carry):
        rf = pl.ds(pl.multiple_of(ci * HG_CHUNK, HG_CHUNK), HG_CHUNK)
        rb = pl.ds(pl.multiple_of((nchunk - 1 - ci) * HG_CHUNK, HG_CHUNK), HG_CHUNK)
        for hh in heads:
            c = hcol[hh]
            o_f, st_f = _hgrn_chunk(q_ref[rf, c].astype(F32), zf_ref[rf, c], v_ref[rf, c], lb_ref[0, 0, hh],
                                    tri_ref[0], [msk_ref[0, li] for li in range(n_lev)], st_ref[2 * hh], False)
            o_b, st_b = _hgrn_chunk(q_ref[rb, c].astype(F32), zb_ref[rb, c], v_ref[rb, c], lb_ref[0, 1, hh],
                                    tri_ref[1], [msk_ref[1, li] for li in range(n_lev)], st_ref[2 * hh + 1], True)
            st_ref[2 * hh] = st_f
            st_ref[2 * hh + 1] = st_b
            of_ref[rf, c] = o_f
            ob_ref[rb, c] = o_b
        return carry

    lax.fori_loop(0, nchunk, body, 0)

    nw = nw_ref[0]

    def finish(t, carry):
        rows = pl.ds(pl.multiple_of(t * HG_FIN_TILE, HG_FIN_TILE), HG_FIN_TILE)
        for hh in heads:
            c = hcol[hh]
            o = of_ref[rows, c] + ob_ref[rows, c]
            o = o * lax.rsqrt(jnp.mean(o * o, axis=-1, keepdims=True) + LN_EPS) * nw
            o_ref[rows, c] = (o * _silu(g_ref[rows, c].astype(F32))).astype(BF16)
        return carry

    lax.fori_loop(0, seq_len // HG_FIN_TILE, finish, 0)
    if emit_state:
        for hh in heads:
            for d in (0, 1):
                sfin_ref[0, d, hh] = st_ref[2 * hh + d].T


def _hgrn_consts():
    c = HG_CHUNK
    t = np.arange(c)[:, None]
    s = np.arange(c)[None, :]
    tril = (s <= t)
    tri = jnp.asarray(np.stack([tril, tril.T]), BF16)
    fwd = [(t // HG_LEVELS[0] == s // HG_LEVELS[0]) & (s <= t)]
    for blk in HG_LEVELS[1:]:
        fwd.append((t // blk == s // blk) & (t % blk >= blk // 2) & (s % blk < blk // 2))
    fwd = np.stack(fwd)
    masks = jnp.asarray(np.stack([fwd, np.swapaxes(fwd, 1, 2)]), F32)
    return tri, masks


def _hgrn_mixer(pa, pb, lower, norm_w, tri, masks, layer, *, n_seq, seq_len, row_block0, s0):
    zero_init = s0 is None
    wide = HG_HEADS * DK
    seq_spec = lambda cb: pl.BlockSpec((seq_len, wide), lambda b, h: (row_block0 + b, cb // HG_HEADS + h))
    in_specs = [
        seq_spec(CB_HQ), seq_spec(CB_HI), seq_spec(CB_HG), seq_spec(CB_HFF), seq_spec(CB_HFB),
        pl.BlockSpec((1, 2, HG_HEADS, 1, DK), lambda b, h: (layer, 0, h, 0, 0)),
        pl.BlockSpec((1, 1, DK), lambda b, h: (layer, 0, 0)),
        pl.BlockSpec((2, HG_CHUNK, HG_CHUNK), lambda b, h: (0, 0, 0)),
        pl.BlockSpec((2, len(HG_LEVELS), HG_CHUNK, HG_CHUNK), lambda b, h: (0, 0, 0, 0)),
    ]
    args = [pa, pa, pa, pb, pb, lower, norm_w, tri, masks]
    if not zero_init:
        in_specs.append(pl.BlockSpec((1, 1, 2, HG_HEADS, DK, DK), lambda b, h: (b, layer, 0, h, 0, 0)))
        args.append(s0)
    out_specs = [pl.BlockSpec((seq_len, wide), lambda b, h: (b, h))]
    out_shape = [jax.ShapeDtypeStruct((n_seq * seq_len, BRANCH_W), BF16)]
    if zero_init:
        out_specs.append(pl.BlockSpec((1, 2, HG_HEADS, DK, DK), lambda b, h: (b, 0, h, 0, 0)))
        out_shape.append(jax.ShapeDtypeStruct((n_seq, 2, H_C, DK, DK), F32))
    res = pl.pallas_call(
        functools.partial(_hgrn_kernel, seq_len=seq_len, zero_init=zero_init, emit_state=zero_init),
        grid=(n_seq, H_C // HG_HEADS),
        in_specs=in_specs,
        out_specs=out_specs,
        out_shape=out_shape,
        scratch_shapes=[pltpu.VMEM((seq_len, wide), F32), pltpu.VMEM((seq_len, wide), F32),
                        pltpu.VMEM((2 * HG_HEADS, DK, DK), F32)],
        compiler_params=_params(("arbitrary", "arbitrary")),
        name="hgrn2_ctx" if zero_init else "hgrn2_latent",
    )(*args)
    return res if zero_init else (res[0], None)


def _merge_kernel(ya_ref, ybc_ref, ybl_ref, ycc_ref, ycl_ref, ga_ref, gb_ref, gc_ref, w_ref, o_ref, *, n_ctx_tiles):
    is_ctx = pl.program_id(1) < n_ctx_tiles
    yb = jnp.where(is_ctx, ybc_ref[...], ybl_ref[...])
    yc = jnp.where(is_ctx, ycc_ref[...], ycl_ref[...])
    acc = _sigmoid(ga_ref[...].astype(F32)) * _dot(ya_ref[...], w_ref[0, 0])
    acc = acc + _sigmoid(gb_ref[...].astype(F32)) * _dot(yb, w_ref[0, 1])
    acc = acc + _sigmoid(gc_ref[...].astype(F32)) * _dot(yc, w_ref[0, 2])
    o_ref[...] = acc.astype(BF16)


def _merge_branches(ya, yb_ctx, yb_lat, yc_ctx, yc_lat, pa, wb_bf, layer):
    tm, tn = 512, 1024
    nj = D_MODEL // tn
    nc = T_CTX // tm
    gate0 = CB_GATES * LANE // tn
    y_spec = pl.BlockSpec((tm, BRANCH_W), lambda j, i: (i, 0))
    ctx_spec = pl.BlockSpec((tm, BRANCH_W), lambda j, i: (jnp.minimum(i, nc - 1), 0))
    lat_spec = pl.BlockSpec((tm, BRANCH_W), lambda j, i: (jnp.maximum(i - nc, 0), 0))
    gate_spec = lambda k: pl.BlockSpec((tm, tn), lambda j, i: (i, gate0 + k * nj + j))
    return pl.pallas_call(
        functools.partial(_merge_kernel, n_ctx_tiles=nc),
        grid=(nj, T_ALL // tm),
        in_specs=[y_spec, ctx_spec, lat_spec, ctx_spec, lat_spec, gate_spec(0), gate_spec(1), gate_spec(2),
                  pl.BlockSpec((1, 3, BRANCH_W, tn), lambda j, i: (layer, 0, 0, j))],
        out_specs=pl.BlockSpec((tm, tn), lambda j, i: (i, j)),
        out_shape=jax.ShapeDtypeStruct((T_ALL, D_MODEL), BF16),
        compiler_params=_params(("arbitrary", "arbitrary")),
        name="merge_branches",
    )(ya, yb_ctx, yb_lat, yc_ctx, yc_lat, pa, pa, pa, wb_bf)


def _out_router_kernel(mg_ref, x_ref, mod_ref, wo_ref, lnw_ref, lnb_ref, rwh_ref, rwl_ref, rb_ref,
                       x1_ref, h2_ref, rt_ref, *, alpha):
    mix = _dot(mg_ref[...], wo_ref[0])
    g1 = mod_ref[0, 0, 2:3, :]
    x1 = _layer_norm_rows(alpha * x_ref[...] + g1 * mix, lnw_ref[0], lnb_ref[0])
    x1_ref[...] = x1
    h2 = x1 * (1.0 + mod_ref[0, 0, 4:5, :]) + mod_ref[0, 0, 3:4, :]
    h2_ref[...] = h2

    hh = h2.astype(BF16)
    hl = (h2 - hh.astype(F32)).astype(BF16)
    lg = _dot(hh, rwh_ref[0]) + _dot(hl, rwh_ref[0]) + _dot(hh, rwl_ref[0]) + rb_ref[0]

    lane = lax.broadcasted_iota(jnp.int32, lg.shape, 1)
    neg = jnp.float32(-jnp.inf)
    big = jnp.int32(1 << 20)
    is_c = lane < N_GROUPS
    cmax = jnp.max(jnp.where(is_c, lg, neg), axis=-1, keepdims=True)
    g_idx = jnp.min(jnp.where(is_c & (lg == cmax), lane, big), axis=-1, keepdims=True)
    p_group = 1.0 / jnp.sum(jnp.where(is_c, jnp.exp(lg - cmax), 0.0), axis=-1, keepdims=True)
    lo_lane = N_GROUPS + EXPERTS_PER_GROUP * g_idx
    in_g = (lane >= lo_lane) & (lane < lo_lane + EXPERTS_PER_GROUP)
    v1 = jnp.max(jnp.where(in_g, lg, neg), axis=-1, keepdims=True)
    i1 = jnp.min(jnp.where(in_g & (lg == v1), lane, big), axis=-1, keepdims=True)
    rest = in_g & (lane != i1)
    v2 = jnp.max(jnp.where(rest, lg, neg), axis=-1, keepdims=True)
    i2 = jnp.min(jnp.where(rest & (lg == v2), lane, big), axis=-1, keepdims=True)
    e21 = jnp.exp(v2 - v1)
    w1 = p_group / (1.0 + e21)
    w2 = p_group * e21 / (1.0 + e21)
    e1 = (i1 - N_GROUPS).astype(F32)
    e2 = (i2 - N_GROUPS).astype(F32)
    rt_ref[...] = jnp.where(lane == 0, e1, jnp.where(lane == 1, e2, jnp.where(lane == 2, w1, jnp.where(lane == 3, w2, 0.0))))


def _out_proj_router(merged, x, mods, wo_bf, ln_w, ln_b, rw_hi, rw_lo, rb, layer, alpha):
    tm = 256
    row = pl.BlockSpec((tm, D_MODEL), lambda i: (i, 0))
    vec = pl.BlockSpec((1, 1, D_MODEL), lambda i: (layer, 0, 0))
    rw_spec = pl.BlockSpec((1, D_MODEL, ROUTE_LANES), lambda i: (layer, 0, 0))
    return pl.pallas_call(
        functools.partial(_out_router_kernel, alpha=alpha),
        grid=(T_ALL // tm,),
        in_specs=[
            row, row, _mod_spec(layer, tm),
            pl.BlockSpec((1, D_MODEL, D_MODEL), lambda i: (layer, 0, 0)),
            vec, vec, rw_spec, rw_spec,
            pl.BlockSpec((1, 1, ROUTE_LANES), lambda i: (layer, 0, 0)),
        ],
        out_specs=[row, row, pl.BlockSpec((tm, ROUTE_LANES), lambda i: (i, 0))],
        out_shape=[jax.ShapeDtypeStruct((T_ALL, D_MODEL), F32), jax.ShapeDtypeStruct((T_ALL, D_MODEL), F32),
                   jax.ShapeDtypeStruct((T_ALL, ROUTE_LANES), F32)],
        compiler_params=_params(("arbitrary",)),
        name="out_proj_router",
    )(merged, x, mods, wo_bf, ln_w, ln_b, rw_hi, rw_lo, rb)


def _row_copy(src_hbm, src_row, dst_ref, dst_row, sem):
    return pltpu.make_async_copy(src_hbm.at[pl.ds(src_row, 1)], dst_ref.at[pl.ds(dst_row, 1)], sem)


def _row_gather(src_hbm, idx_of, dst_ref, sem, n_rows):
    def issue(r, carry):
        _row_copy(src_hbm, idx_of(r), dst_ref, r, sem).start()
        return carry
    lax.fori_loop(0, n_rows, issue, 0, unroll=ROW_ISSUE_UNROLL)


def _row_gather_unrolled(src_hbm, idx_of, dst_ref, sem, n_rows):
    for r in range(n_rows):
        _row_copy(src_hbm, idx_of(r), dst_ref, r, sem).start()


def _row_gather_wait(src_hbm, dst_ref, sem, n_rows):
    pltpu.make_async_copy(src_hbm.at[pl.ds(0, n_rows)], dst_ref, sem).wait()


def _moe_kernel(be_ref, nused_ref, tok_ref, tokn_ref, h2_hbm, w1_ref, w3_ref, w2_ref, rw_ref, o_ref, xbuf, sem):
    del be_ref
    i = pl.program_id(0)
    n_used = nused_ref[0]
    slot = lax.rem(i, 2)

    @pl.when((i == 0) & (n_used > 0))
    def _():
        _row_gather(h2_hbm, lambda r: tok_ref[0, 0, r], xbuf.at[0], sem.at[0], MOE_BLK)

    @pl.when(i < n_used)
    def _():
        _row_gather_wait(h2_hbm, xbuf.at[slot], sem.at[slot], MOE_BLK)
        _row_gather_unrolled(h2_hbm, lambda r: tokn_ref[0, 0, r], xbuf.at[1 - slot], sem.at[1 - slot], MOE_BLK)
        xb = xbuf[slot].astype(BF16)
        hid = _silu(_dot(xb, w1_ref[0, 0])) * _dot(xb, w3_ref[0, 0])
        o_ref[...] = _dot(hid.astype(BF16), w2_ref[0, 0]) * rw_ref[...]

    @pl.when(i + 1 == n_used)
    def _():
        _row_gather_wait(h2_hbm, xbuf.at[1 - slot], sem.at[1 - slot], MOE_BLK)

    @pl.when(i >= n_used)
    def _():
        o_ref[...] = jnp.zeros(o_ref.shape, F32)


def _routed_experts(h2, block_expert, n_used, row_tok, row_w, w1_bf, w3_bf, w2_bf, layer):
    tok3 = row_tok.reshape(N_MOE_BLOCKS, 1, MOE_BLK)
    grid_spec = pltpu.PrefetchScalarGridSpec(
        num_scalar_prefetch=2,
        grid=(N_MOE_BLOCKS,),
        in_specs=[
            pl.BlockSpec((1, 1, MOE_BLK), lambda i, be, nu: (i, 0, 0), memory_space=pltpu.SMEM),
            pl.BlockSpec((1, 1, MOE_BLK), lambda i, be, nu: (jnp.minimum(i + 1, N_MOE_BLOCKS - 1), 0, 0),
                         memory_space=pltpu.SMEM),
            pl.BlockSpec(memory_space=pl.ANY),
            pl.BlockSpec((1, 1, D_MODEL, D_EXPERT), lambda i, be, nu: (layer, be[i], 0, 0)),
            pl.BlockSpec((1, 1, D_MODEL, D_EXPERT), lambda i, be, nu: (layer, be[i], 0, 0)),
            pl.BlockSpec((1, 1, D_EXPERT, D_MODEL), lambda i, be, nu: (layer, be[i], 0, 0)),
            pl.BlockSpec((MOE_BLK, 1), lambda i, be, nu: (i, 0)),
        ],
        out_specs=pl.BlockSpec((MOE_BLK, D_MODEL), lambda i, be, nu: (i, 0)),
        scratch_shapes=[pltpu.VMEM((2, MOE_BLK, D_MODEL), F32), pltpu.SemaphoreType.DMA((2,))],
    )
    return pl.pallas_call(
        _moe_kernel,
        grid_spec=grid_spec,
        out_shape=jax.ShapeDtypeStruct((N_MOE_ROWS, D_MODEL), F32),
        compiler_params=_params(("arbitrary",)),
        name="routed_experts",
    )(block_expert, n_used, tok3, tok3, h2, w1_bf, w3_bf, w2_bf, row_w.reshape(N_MOE_ROWS, 1))


def _combine_kernel(*refs, alpha, tm, emit_h):
    if emit_h:
        pos_ref, posn_ref, x1_ref, mod_ref, modn_ref, lnw_ref, lnb_ref, yb_hbm, o_ref, h_ref, buf, sem = refs
    else:
        pos_ref, posn_ref, x1_ref, mod_ref, lnw_ref, lnb_ref, yb_hbm, o_ref, buf, sem = refs
    i = pl.program_id(0)
    slot = lax.rem(i, 2)

    @pl.when(i == 0)
    def _():
        for k in range(TOP_K):
            _row_gather(yb_hbm, lambda r, k=k: pos_ref[0, k, r], buf.at[k], sem.at[k], tm)

    for k in range(TOP_K):
        _row_gather_wait(yb_hbm, buf.at[TOP_K * slot + k], sem.at[TOP_K * slot + k], tm)
    for k in range(TOP_K):
        nxt = TOP_K * (1 - slot) + k
        _row_gather_unrolled(yb_hbm, lambda r, k=k: posn_ref[0, k, r], buf.at[nxt], sem.at[nxt], tm)
    g2 = mod_ref[0, 0, 5:6, :]
    y = alpha * x1_ref[...] + g2 * (buf[TOP_K * slot] + buf[TOP_K * slot + 1])
    x2 = _layer_norm_rows(y, lnw_ref[0], lnb_ref[0])
    o_ref[...] = x2
    if emit_h:
        h_ref[...] = (x2 * (1.0 + modn_ref[0, 0, 1:2, :]) + modn_ref[0, 0, 0:1, :]).astype(BF16)

    @pl.when(i + 1 == pl.num_programs(0))
    def _():
        for k in range(TOP_K):
            nxt = TOP_K * (1 - slot) + k
            _row_gather_wait(yb_hbm, buf.at[nxt], sem.at[nxt], tm)


def _combine_norm(pos, x1, mods, ln_w, ln_b, yb, layer, alpha):
    tm = COMBINE_TILE
    n_tiles = T_ALL // tm
    emit_h = layer + 1 < DEPTH
    row = pl.BlockSpec((tm, D_MODEL), lambda i: (i, 0))
    vec = pl.BlockSpec((1, 1, D_MODEL), lambda i: (layer, 0, 0))
    in_specs = [
        pl.BlockSpec((1, TOP_K, tm), lambda i: (i, 0, 0), memory_space=pltpu.SMEM),
        pl.BlockSpec((1, TOP_K, tm), lambda i: (jnp.minimum(i + 1, n_tiles - 1), 0, 0), memory_space=pltpu.SMEM),
        row, _mod_spec(layer, tm),
    ]
    args = [pos, pos, x1, mods]
    if emit_h:
        in_specs.append(_mod_spec(layer + 1, tm))
        args.append(mods)
    in_specs += [vec, vec, pl.BlockSpec(memory_space=pl.ANY)]
    args += [ln_w, ln_b, yb]
    out_specs = [row]
    out_shape = [jax.ShapeDtypeStruct((T_ALL, D_MODEL), F32)]
    if emit_h:
        out_specs.append(row)
        out_shape.append(jax.ShapeDtypeStruct((T_ALL, D_MODEL), BF16))
    res = pl.pallas_call(
        functools.partial(_combine_kernel, alpha=alpha, tm=tm, emit_h=emit_h),
        grid=(n_tiles,),
        in_specs=in_specs,
        out_specs=out_specs,
        out_shape=out_shape,
        scratch_shapes=[pltpu.VMEM((2 * TOP_K, tm, D_MODEL), F32), pltpu.SemaphoreType.DMA((2 * TOP_K,))],
        compiler_params=_params(("arbitrary",)),
        name="combine_norm",
    )(*args)
    return (res[0], res[1]) if emit_h else (res[0], None)


def _dispatch_plan(route):
    ids = route[:, 0:TOP_K].astype(jnp.int32)
    wts = route[:, TOP_K:2 * TOP_K]
    flat_e = ids.reshape(-1)
    flat_w = wts.reshape(-1)
    onehot = (flat_e[:, None] == jnp.arange(N_EXPERTS, dtype=jnp.int32)[None, :]).astype(jnp.int32)
    grp = 256
    oh3 = onehot.reshape(-1, grp, N_EXPERTS)
    tril = jnp.asarray(np.tril(np.ones((grp, grp), np.float32)))
    inner = jnp.einsum('ts,gse->gte', tril, oh3.astype(F32), precision=lax.Precision.HIGHEST).astype(jnp.int32)
    gsum = jnp.sum(oh3, axis=1)
    goff = jnp.cumsum(gsum, axis=0) - gsum
    csum = (inner + goff[:, None, :]).reshape(-1, N_EXPERTS)
    rank = jnp.sum(csum * onehot, axis=1) - 1
    counts = jnp.sum(gsum, axis=0)
    padded = (counts + MOE_BLK - 1) // MOE_BLK * MOE_BLK
    pad_end = jnp.cumsum(padded)
    pad_start = pad_end - padded
    dest = (jnp.sum(onehot * pad_start[None, :], axis=1) + rank).astype(jnp.int32)
    tok = jnp.arange(T_ALL * TOP_K, dtype=jnp.int32) // TOP_K
    fields = jnp.stack([tok, lax.bitcast_convert_type(flat_w, jnp.int32)], axis=1)
    rows = jnp.zeros((N_MOE_ROWS, 2), jnp.int32).at[dest].set(fields)
    row_tok = rows[:, 0]
    row_w = lax.bitcast_convert_type(rows[:, 1], F32)
    blk_row0 = jnp.arange(N_MOE_BLOCKS, dtype=jnp.int32) * MOE_BLK
    block_expert = jnp.clip(jnp.sum((pad_end[None, :] <= blk_row0[:, None]).astype(jnp.int32), axis=1),
                            0, N_EXPERTS - 1).astype(jnp.int32)
    n_used = (pad_end[-1] // MOE_BLK).astype(jnp.int32).reshape(1)
    tm = COMBINE_TILE
    pos = dest.reshape(T_ALL // tm, tm, TOP_K).transpose(0, 2, 1)
    return block_expert, n_used, row_tok, row_w, pos


def _rope_tables():
    pos = np.arange(DEC_SEQ)
    row = (pos // GRID_W).astype(np.float32)
    col = (pos % GRID_W).astype(np.float32)
    n_freq = DH_B // 4
    inv = jnp.asarray(ROPE_BASE, F32) ** (-jnp.arange(n_freq, dtype=F32) / n_freq)
    ang = jnp.concatenate([jnp.asarray(row)[:, None] * inv, jnp.asarray(col)[:, None] * inv], axis=-1)
    cos, sin = jnp.cos(ang), jnp.sin(ang)
    return jnp.concatenate([cos, cos], axis=-1), jnp.concatenate([-sin, sin], axis=-1)


def _na_table_selectors():
    rows = DEC_SEQ // GRID_W
    n_groups = rows // NA_GROUP
    n_dy = 2 * MAX_KH - 1
    sel = np.zeros((3 * NA_GROUP, NA_UNION, n_dy), np.float32)
    for kind, g_rep in enumerate((0, 1, n_groups - 1)):
        u0 = min(max(g_rep * NA_GROUP - MAX_KH // 2, 0), rows - NA_UNION)
        for j in range(NA_GROUP):
            r = g_rep * NA_GROUP + j
            start = min(max(r - MAX_KH // 2, 0), rows - MAX_KH)
            for i in range(NA_UNION):
                if start <= u0 + i < start + MAX_KH:
                    sel[kind * NA_GROUP + j, i, u0 + i - r + MAX_KH - 1] = 1.0
    cols = np.arange(GRID_W)
    col_start = np.clip(cols - KW // 2, 0, GRID_W - KW)
    col_mask = (cols[None, :] >= col_start[:, None]) & (cols[None, :] < col_start[:, None] + KW)
    dx_idx = np.clip(cols[None, :] - cols[:, None] + KW - 1, 0, 2 * KW - 2)
    dx_onehot = (dx_idx[:, :, None] == np.arange(2 * KW - 1)[None, None, :]).astype(np.float32)
    valid = (sel.sum(-1) > 0)[:, None, None, :, None] & col_mask[None, None, :, None, :]
    return sel, dx_onehot, valid


def _na_bias_table(rpb):
    sel, dx_onehot, valid = _na_table_selectors()
    toep = jnp.einsum('qkx,hyx->hyqk', jnp.asarray(dx_onehot), rpb, precision=lax.Precision.HIGHEST)
    tab = jnp.einsum('viy,hyqk->vhqik', jnp.asarray(sel), toep, precision=lax.Precision.HIGHEST)
    tab = jnp.where(jnp.asarray(valid), tab, MASK_VALUE)
    return tab.reshape(3 * NA_GROUP, H_B, GRID_W, NA_UNION * GRID_W)


def kernel(x_prompt, x_sample, cache_k, cache_v, state_hgrn, c, c_ctx, w_mod, b_mod, w_in, gmlp_ws, gmlp_bs, na_rpb, hgrn_lb, hgrn_norm_w, w_branch, w_o, ln1_w, ln1_b, ln2_w, ln2_b, router_w1, router_b1, router_w2, router_b2, moe_w1, moe_w3, moe_w2):
    alpha = (2.0 * DEPTH) ** 0.25
    cond = jnp.concatenate([c_ctx[None, :], c, jnp.zeros((MOD_ROWS - 1 - DEC_BATCH, D_MODEL), F32)], axis=0)
    mods = _modulation(cond, w_mod, b_mod)

    p_lb = jax.nn.softmax(hgrn_lb.astype(F32), axis=0)
    lower = (jnp.cumsum(p_lb, axis=0) - p_lb[0:1]).reshape(DEPTH, 2, H_C, 1, DK)
    norm_w = hgrn_norm_w.reshape(DEPTH, 1, DK)
    tri, masks = _hgrn_consts()
    cos_t, sin_t = _rope_tables()

    w_in_bf = w_in.astype(BF16)
    ws_bf = gmlp_ws.astype(BF16)
    bs_b = jnp.repeat(jnp.swapaxes(gmlp_bs, 1, 2), LANE, axis=2)
    wb_bf = w_branch.astype(BF16)
    wo_bf = w_o.astype(BF16)
    rw = jnp.concatenate([router_w1, router_w2.transpose(0, 2, 1, 3).reshape(DEPTH, D_MODEL, N_EXPERTS),
                          jnp.zeros((DEPTH, D_MODEL, ROUTE_LANES - N_GROUPS - N_EXPERTS), F32)], axis=-1)
    rw_hi = rw.astype(BF16)
    rw_lo = (rw - rw_hi.astype(F32)).astype(BF16)
    rb = jnp.concatenate([router_b1, router_b2.reshape(DEPTH, N_EXPERTS),
                          jnp.zeros((DEPTH, ROUTE_LANES - N_GROUPS - N_EXPERTS), F32)], axis=-1).reshape(DEPTH, 1, ROUTE_LANES)
    w1_bf = moe_w1.astype(BF16)
    w3_bf = moe_w3.astype(BF16)
    w2_bf = moe_w2.astype(BF16)
    ln1w, ln1b = ln1_w.reshape(DEPTH, 1, D_MODEL), ln1_b.reshape(DEPTH, 1, D_MODEL)
    ln2w, ln2b = ln2_w.reshape(DEPTH, 1, D_MODEL), ln2_b.reshape(DEPTH, 1, D_MODEL)

    x, h = _stack_modulate(x_prompt, x_sample, mods)
    ks_out, vs_out, ss_out = [], [], []
    for l in range(DEPTH):
        pa, pb = _in_projection(h, w_in_bf, l)
        ya = _gmlp_mixer(pa, ws_bf, bs_b, l)
        yb_ctx = _context_attention(pa)
        yb_lat = _neighbourhood_attention(pa, cache_k, cache_v, cos_t, sin_t, _na_bias_table(na_rpb[l]), l)
        yc_ctx, s_ctx = _hgrn_mixer(pa, pb, lower, norm_w, tri, masks, l,
                                    n_seq=BATCH, seq_len=SEQ, row_block0=0, s0=None)
        yc_lat, _ = _hgrn_mixer(pa, pb, lower, norm_w, tri, masks, l,
                                n_seq=DEC_BATCH, seq_len=DEC_SEQ, row_block0=T_CTX // DEC_SEQ, s0=state_hgrn)
        merged = _merge_branches(ya, yb_ctx, yb_lat, yc_ctx, yc_lat, pa, wb_bf, l)
        x1, h2, route = _out_proj_router(merged, x, mods, wo_bf, ln1w, ln1b, rw_hi, rw_lo, rb, l, alpha)
        block_expert, n_used, row_tok, row_w, pos = _dispatch_plan(route)
        ybk = _routed_experts(h2, block_expert, n_used, row_tok, row_w, w1_bf, w3_bf, w2_bf, l)
        x, h = _combine_norm(pos, x1, mods, ln2w, ln2b, ybk, l, alpha)

        kv = pa[:T_CTX, CB_K * LANE:(CB_V + H_B) * LANE].astype(F32)
        ks_out.append(kv[:, :BRANCH_W].reshape(BATCH, SEQ, H_B, DH_B))
        vs_out.append(kv[:, BRANCH_W:].reshape(BATCH, SEQ, H_B, DH_B))
        ss_out.append(s_ctx)

    y_prompt = x[:T_CTX].reshape(BATCH, SEQ, D_MODEL)
    y_sample = x[T_CTX:].reshape(DEC_BATCH, DEC_SEQ, D_MODEL)
    return (y_prompt, y_sample, jnp.stack(ks_out, axis=1), jnp.stack(vs_out, axis=1), jnp.stack(ss_out, axis=1))
```

```python
import functools

import numpy as np
import jax
import jax.numpy as jnp
from jax import lax
from jax.experimental import pallas as pl
from jax.experimental.pallas import tpu as pltpu

F32 = jnp.float32
BF16 = jnp.bfloat16

D_MODEL = 2048
BATCH = 16
SEQ = 256
DEPTH = 2
DEC_BATCH = 4
DEC_SEQ = 4096
PAST_LEN = 512
GRID_W = 64
BRANCH_W = D_MODEL // 2
CHUNK_A = 128
G_A = 8
DH_B = 128
H_B = 8
MAX_KH = 8
KW = 16
ROPE_BASE = 10000.0
DK = 128
H_C = 8
N_GROUPS = 4
EXPERTS_PER_GROUP = 8
N_EXPERTS = 32
TOP_K = 2
D_EXPERT = D_MODEL // 2
LN_EPS = 1e-5
MASK_VALUE = -1e30
F_MIN = 1e-6
IN_COLS = 16384

T_CTX = BATCH * SEQ
T_LAT = DEC_BATCH * DEC_SEQ
T_ALL = T_CTX + T_LAT
SLAB = 4096
MOD_ROWS = 8
LANE = 128

PROJ_TN = 2048
PB_TILE = 3
PA_COLS = IN_COLS - PROJ_TN
CB_Q, CB_K, CB_V = 16, 24, 32
CB_HQ, CB_HI, CB_HG = 40, 48, 56
CB_GATES = 64
CB_HFF, CB_HFB = 0, 8

HG_CHUNK = 128
HG_LEVELS = (8, 16, 32, 64, 128)
HG_FIN_TILE = 256
HG_HEADS = 2
NA_CTX_TILE = 512
NA_GROUP = 4
NA_UNION = NA_GROUP + MAX_KH
MOE_BLK = 256
MOE_ISSUE_CHUNKS = 4
N_MOE_ROWS = T_ALL * TOP_K + N_EXPERTS * MOE_BLK
N_MOE_BLOCKS = N_MOE_ROWS // MOE_BLK
ROUTE_LANES = 128
COMBINE_TILE = 256
ROW_ISSUE_UNROLL = 8

VMEM_LIMIT = 56 * 1024 * 1024


def _params(sem):
    return pltpu.CompilerParams(dimension_semantics=sem, vmem_limit_bytes=VMEM_LIMIT)


def _sigmoid(x):
    return 1.0 / (1.0 + jnp.exp(-x))


def _silu(x):
    return x * _sigmoid(x)


def _gelu_tanh(x):
    return x * (0.5 * (1.0 + jnp.tanh(0.7978845608028654 * (x + 0.044715 * (x * x * x)))))


def _layer_norm_rows(y, w, b):
    mu = jnp.mean(y, axis=-1, keepdims=True)
    yc = y - mu
    var = jnp.mean(yc * yc, axis=-1, keepdims=True)
    return yc * lax.rsqrt(var + LN_EPS) * w + b


def _dot(a, b):
    return jnp.dot(a, b, preferred_element_type=F32)


def _dot_nt(a, b):
    return lax.dot_general(a, b, (((1,), (1,)), ((), ())), preferred_element_type=F32)


def _dot_tn(a, b):
    return lax.dot_general(a, b, (((0,), (0,)), ((), ())), preferred_element_type=F32)


def _mod_kernel(c_ref, w_ref, b_ref, o_ref):
    o_ref[0] = _dot(_silu(c_ref[...]), w_ref[0]) + b_ref[0]


def _modulation(cond, w_mod, b_mod):
    tn = 1024
    n = 6 * D_MODEL
    out = pl.pallas_call(
        _mod_kernel,
        grid=(DEPTH, n // tn),
        in_specs=[
            pl.BlockSpec((MOD_ROWS, D_MODEL), lambda l, j: (0, 0)),
            pl.BlockSpec((1, D_MODEL, tn), lambda l, j: (l, 0, j)),
            pl.BlockSpec((1, 1, tn), lambda l, j: (l, 0, j)),
        ],
        out_specs=pl.BlockSpec((1, MOD_ROWS, tn), lambda l, j: (l, 0, j)),
        out_shape=jax.ShapeDtypeStruct((DEPTH, MOD_ROWS, n), F32),
        compiler_params=_params(("arbitrary", "arbitrary")),
        name="modulation",
    )(cond, w_mod, b_mod.reshape(DEPTH, 1, n))
    return out.reshape(DEPTH, MOD_ROWS, 6, D_MODEL)


def _mod_spec(layer, tm, grid_rank=1):
    if grid_rank == 1:
        return pl.BlockSpec((1, 1, 6, D_MODEL), lambda i: (layer, (i * tm) // SLAB, 0, 0))
    return pl.BlockSpec((1, 1, 6, D_MODEL), lambda j, i: (layer, (i * tm) // SLAB, 0, 0))


def _stack_kernel(xp_ref, xs_ref, mod_ref, x_ref, h_ref, *, n_ctx_tiles):
    x = jnp.where(pl.program_id(0) < n_ctx_tiles, xp_ref[...], xs_ref[...])
    x_ref[...] = x
    h_ref[...] = (x * (1.0 + mod_ref[0, 0, 1:2, :]) + mod_ref[0, 0, 0:1, :]).astype(BF16)


def _stack_modulate(x_prompt, x_sample, mods):
    tm = 512
    nc = T_CTX // tm
    row = pl.BlockSpec((tm, D_MODEL), lambda i: (i, 0))
    return pl.pallas_call(
        functools.partial(_stack_kernel, n_ctx_tiles=nc),
        grid=(T_ALL // tm,),
        in_specs=[
            pl.BlockSpec((tm, D_MODEL), lambda i: (jnp.minimum(i, nc - 1), 0)),
            pl.BlockSpec((tm, D_MODEL), lambda i: (jnp.maximum(i - nc, 0), 0)),
            _mod_spec(0, tm),
        ],
        out_specs=[row, row],
        out_shape=[jax.ShapeDtypeStruct((T_ALL, D_MODEL), F32), jax.ShapeDtypeStruct((T_ALL, D_MODEL), BF16)],
        compiler_params=_params(("arbitrary",)),
        name="stack_modulate",
    )(x_prompt.reshape(T_CTX, D_MODEL), x_sample.reshape(T_LAT, D_MODEL), mods)


def _inproj_kernel(h_ref, w_ref, o_ref):
    o_ref[...] = _dot(h_ref[...], w_ref[0]).astype(o_ref.dtype)


def _in_projection(h, w_in_bf, layer):
    tm, tn = 1024, PROJ_TN
    n_tiles = IN_COLS // tn

    def call(col_tile_of, n_out_tiles, dtype, name):
        return pl.pallas_call(
            _inproj_kernel,
            grid=(n_out_tiles, T_ALL // tm),
            in_specs=[
                pl.BlockSpec((tm, D_MODEL), lambda j, i: (i, 0)),
                pl.BlockSpec((1, D_MODEL, tn), lambda j, i: (layer, 0, col_tile_of(j))),
            ],
            out_specs=pl.BlockSpec((tm, tn), lambda j, i: (i, j)),
            out_shape=jax.ShapeDtypeStruct((T_ALL, n_out_tiles * tn), dtype),
            compiler_params=_params(("arbitrary", "arbitrary")),
            name=name,
        )(h, w_in_bf)

    pa = call(lambda j: j + (j >= PB_TILE).astype(jnp.int32), n_tiles - 1, BF16, "in_projection_a")
    pb = call(lambda j: j + PB_TILE, 1, F32, "in_projection_b")
    return pa, pb


def _gmlp_kernel(uv_ref, ws_ref, bs_ref, o_ref, *, tm):
    ge = _gelu_tanh(uv_ref[...].astype(F32))
    u = ge[:, :BRANCH_W]
    v = ge[:, BRANCH_W:]
    mu = jnp.mean(v, axis=-1, keepdims=True)
    vc = v - mu
    var = jnp.mean(vc * vc, axis=-1, keepdims=True)
    vn = (vc * lax.rsqrt(var + LN_EPS)).astype(BF16)
    for n in range(tm // CHUNK_A):
        r = slice(n * CHUNK_A, (n + 1) * CHUNK_A)
        for g in range(G_A):
            c = slice(g * LANE, (g + 1) * LANE)
            sv = _dot(ws_ref[0, g], vn[r, c]) + bs_ref[0, :, c]
            o_ref[r, c] = (u[r, c] * sv).astype(BF16)


def _gmlp_mixer(pa, ws_bf, bs_b, layer):
    tm = 256
    return pl.pallas_call(
        functools.partial(_gmlp_kernel, tm=tm),
        grid=(T_ALL // tm,),
        in_specs=[
            pl.BlockSpec((tm, 2 * BRANCH_W), lambda i: (i, 0)),
            pl.BlockSpec((1, G_A, CHUNK_A, CHUNK_A), lambda i: (layer, 0, 0, 0)),
            pl.BlockSpec((1, CHUNK_A, BRANCH_W), lambda i: (layer, 0, 0)),
        ],
        out_specs=pl.BlockSpec((tm, BRANCH_W), lambda i: (i, 0)),
        out_shape=jax.ShapeDtypeStruct((T_ALL, BRANCH_W), BF16),
        compiler_params=_params(("arbitrary",)),
        name="gmlp_mixer",
    )(pa, ws_bf, bs_b)


def _ctx_attn_kernel(q_ref, k_ref, v_ref, o_ref):
    scale = DH_B ** -0.5
    s = _dot_nt(q_ref[...], k_ref[...]) * scale
    m = jnp.max(s, axis=-1, keepdims=True)
    p = jnp.exp(s - m)
    den = jnp.sum(p, axis=-1, keepdims=True)
    o_ref[...] = (_dot(p.astype(BF16), v_ref[...]) / den).astype(BF16)


def _context_attention(pa):
    return pl.pallas_call(
        _ctx_attn_kernel,
        grid=(BATCH, H_B),
        in_specs=[
            pl.BlockSpec((SEQ, DH_B), lambda b, h: (b, CB_Q + h)),
            pl.BlockSpec((SEQ, DH_B), lambda b, h: (b, CB_K + h)),
            pl.BlockSpec((SEQ, DH_B), lambda b, h: (b, CB_V + h)),
        ],
        out_specs=pl.BlockSpec((SEQ, DH_B), lambda b, h: (b, h)),
        out_shape=jax.ShapeDtypeStruct((T_CTX, BRANCH_W), BF16),
        compiler_params=_params(("arbitrary", "arbitrary")),
        name="context_attention",
    )(pa, pa, pa)


def _na_kernel(q_ref, k_ref, v_ref, kc_ref, vc_ref, cos_ref, sin_ref, bias_ref, o_ref,
               qr_ref, kr_ref, octx_ref, mctx_ref, lctx_ref):
    scale = DH_B ** -0.5
    rows = DEC_SEQ // GRID_W
    kh = MAX_KH
    cos = cos_ref[...]
    sin = sin_ref[...]
    q = q_ref[...].astype(F32)
    k = k_ref[...].astype(F32)
    qr_ref[...] = (q * cos + pltpu.roll(q, DH_B // 2, 1) * sin).astype(BF16)
    kr_ref[...] = (k * cos + pltpu.roll(k, DH_B // 2, 1) * sin).astype(BF16)
    kc = kc_ref[0, 0].astype(BF16)
    vc = vc_ref[0, 0].astype(BF16)

    def ctx_step(t, carry):
        trow = pl.ds(pl.multiple_of(t * NA_CTX_TILE, NA_CTX_TILE), NA_CTX_TILE)
        s = _dot_nt(q_ref[trow, :], kc) * scale
        m = jnp.max(s, axis=-1, keepdims=True)
        p = jnp.exp(s - m)
        octx_ref[trow, :] = _dot(p.astype(BF16), vc)
        mctx_ref[trow, :] = jnp.broadcast_to(m, (NA_CTX_TILE, LANE))
        lctx_ref[trow, :] = jnp.broadcast_to(jnp.sum(p, axis=-1, keepdims=True), (NA_CTX_TILE, LANE))
        return carry

    lax.fori_loop(0, DEC_SEQ // NA_CTX_TILE, ctx_step, 0, unroll=4)

    n_groups = rows // NA_GROUP
    nq = NA_GROUP * GRID_W
    nk = NA_UNION * GRID_W

    def group_step(g, carry):
        u0 = jnp.clip(g * NA_GROUP - kh // 2, 0, rows - NA_UNION)
        kind = jnp.where(g == 0, 0, jnp.where(g == n_groups - 1, 2, 1))
        qrow = pl.ds(pl.multiple_of(g * nq, nq), nq)
        krow = pl.ds(pl.multiple_of(u0 * GRID_W, GRID_W), nk)
        bias = jnp.concatenate([bias_ref[kind * NA_GROUP + j, 0] for j in range(NA_GROUP)], axis=0)
        s_loc = _dot_nt(qr_ref[qrow, :], kr_ref[krow, :]) * scale + bias
        m_ctx = mctx_ref[qrow, :][:, 0:1]
        m = jnp.maximum(jnp.max(s_loc, axis=-1, keepdims=True), m_ctx)
        p_loc = jnp.exp(s_loc - m)
        w_ctx = jnp.exp(m_ctx - m)
        den = jnp.sum(p_loc, axis=-1, keepdims=True) + lctx_ref[qrow, :][:, 0:1] * w_ctx
        o = _dot(p_loc.astype(BF16), v_ref[krow, :]) + octx_ref[qrow, :] * w_ctx
        o_ref[qrow, :] = (o / den).astype(BF16)
        return carry

    lax.fori_loop(0, n_groups, group_step, 0, unroll=4)


def _neighbourhood_attention(pa, cache_k, cache_v, cos_t, sin_t, bias_tab, layer):
    ck = cache_k.reshape(DEC_BATCH, DEPTH, PAST_LEN, H_B * DH_B)
    cv = cache_v.reshape(DEC_BATCH, DEPTH, PAST_LEN, H_B * DH_B)
    lat = T_CTX // DEC_SEQ
    return pl.pallas_call(
        _na_kernel,
        grid=(DEC_BATCH, H_B),
        in_specs=[
            pl.BlockSpec((DEC_SEQ, DH_B), lambda b, h: (lat + b, CB_Q + h)),
            pl.BlockSpec((DEC_SEQ, DH_B), lambda b, h: (lat + b, CB_K + h)),
            pl.BlockSpec((DEC_SEQ, DH_B), lambda b, h: (lat + b, CB_V + h)),
            pl.BlockSpec((1, 1, PAST_LEN, DH_B), lambda b, h: (b, layer, 0, h)),
            pl.BlockSpec((1, 1, PAST_LEN, DH_B), lambda b, h: (b, layer, 0, h)),
            pl.BlockSpec((DEC_SEQ, DH_B), lambda b, h: (0, 0)),
            pl.BlockSpec((DEC_SEQ, DH_B), lambda b, h: (0, 0)),
            pl.BlockSpec((3 * NA_GROUP, 1, GRID_W, NA_UNION * GRID_W), lambda b, h: (0, h, 0, 0)),
        ],
        out_specs=pl.BlockSpec((DEC_SEQ, DH_B), lambda b, h: (b, h)),
        out_shape=jax.ShapeDtypeStruct((T_LAT, BRANCH_W), BF16),
        scratch_shapes=[pltpu.VMEM((DEC_SEQ, DH_B), BF16) for _ in range(2)]
        + [pltpu.VMEM((DEC_SEQ, DH_B), F32) for _ in range(3)],
        compiler_params=_params(("arbitrary", "arbitrary")),
        name="neighbourhood_attention",
    )(pa, pa, pa, ck, cv, cos_t, sin_t, bias_tab)


def _ref_rows(a, blk, r):
    pieces = [jnp.broadcast_to(a[b0 + r:b0 + r + 1, :], (blk, a.shape[1])) for b0 in range(0, HG_CHUNK, blk)]
    return pieces[0] if len(pieces) == 1 else jnp.concatenate(pieces, axis=0)


def _hgrn_chunk(q, z, vb, lb, tri, masks, st, backward):
    c = HG_CHUNK
    f = jnp.maximum(lb + (1.0 - lb) * _sigmoid(z), F_MIN)
    gl = jnp.log(f)
    kk = 1.0 - f
    hi = gl.astype(BF16)
    lo = (gl - hi.astype(F32)).astype(BF16)
    a = _dot(tri, hi) + _dot(tri, lo)

    d0 = a - _ref_rows(a, HG_LEVELS[0], HG_LEVELS[0] // 2 if backward else HG_LEVELS[0] // 2 - 1)
    qw = (q * jnp.exp(d0)).astype(BF16)
    kw = (kk * jnp.exp(-d0)).astype(BF16)
    scores = _dot_nt(qw, kw) * masks[0]
    for li, blk in enumerate(HG_LEVELS[1:], start=1):
        dl = a - _ref_rows(a, blk, blk // 2 if backward else blk // 2 - 1)
        w = jnp.exp(-jnp.abs(dl))
        scores = scores + _dot_nt((q * w).astype(BF16), (kk * w).astype(BF16)) * masks[li]
    o = _dot(scores.astype(BF16), vb)
    o = o + _dot_nt((q * jnp.exp(a)).astype(BF16), st.astype(BF16))
    a_tot = a[0:1, :] if backward else a[c - 1:c, :]
    kd = (kk * jnp.exp(a_tot - a)).astype(BF16)
    st_new = st * jnp.exp(a_tot) + _dot_tn(vb, kd)
    return o, st_new


def _hgrn_kernel(*refs, seq_len, zero_init, emit_state):
    q_ref, v_ref, g_ref, zf_ref, zb_ref, lb_ref, nw_ref, tri_ref, msk_ref = refs[:9]
    pos = 9
    s0_ref = None
    if not zero_init:
        s0_ref = refs[pos]
        pos += 1
    o_ref = refs[pos]
    pos += 1
    sfin_ref = None
    if emit_state:
        sfin_ref = refs[pos]
        pos += 1
    of_ref, ob_ref, st_ref = refs[pos:pos + 3]
    nchunk = seq_len // HG_CHUNK
    n_lev = len(HG_LEVELS)
    heads = range(HG_HEADS)
    hcol = [slice(hh * DK, (hh + 1) * DK) for hh in heads]
    for hh in heads:
        for d in (0, 1):
            if zero_init:
                st_ref[2 * hh + d] = jnp.zeros((DK, DK), F32)
            else:
                st_ref[2 * hh + d] = s0_ref[0, 0, d, hh].T

    def body(ci, carry):
        rf = pl.ds(pl.multiple_of(ci * HG_CHUNK, HG_CHUNK), HG_CHUNK)
        rb = pl.ds(pl.multiple_of((nchunk - 1 - ci) * HG_CHUNK, HG_CHUNK), HG_CHUNK)
        for hh in heads:
            c = hcol[hh]
            o_f, st_f = _hgrn_chunk(q_ref[rf, c].astype(F32), zf_ref[rf, c], v_ref[rf, c], lb_ref[0, 0, hh],
                                    tri_ref[0], [msk_ref[0, li] for li in range(n_lev)], st_ref[2 * hh], False)
            o_b, st_b = _hgrn_chunk(q_ref[rb, c].astype(F32), zb_ref[rb, c], v_ref[rb, c], lb_ref[0, 1, hh],
                                    tri_ref[1], [msk_ref[1, li] for li in range(n_lev)], st_ref[2 * hh + 1], True)
            st_ref[2 * hh] = st_f
            st_ref[2 * hh + 1] = st_b
            of_ref[rf, c] = o_f
            ob_ref[rb, c] = o_b
        return carry

    lax.fori_loop(0, nchunk, body, 0, unroll=2)

    nw = nw_ref[0]

    def finish(t, carry):
        rows = pl.ds(pl.multiple_of(t * HG_FIN_TILE, HG_FIN_TILE), HG_FIN_TILE)
        for hh in heads:
            c = hcol[hh]
            o = of_ref[rows, c] + ob_ref[rows, c]
            o = o * lax.rsqrt(jnp.mean(o * o, axis=-1, keepdims=True) + LN_EPS) * nw
            o_ref[rows, c] = (o * _silu(g_ref[rows, c].astype(F32))).astype(BF16)
        return carry

    lax.fori_loop(0, seq_len // HG_FIN_TILE, finish, 0)
    if emit_state:
        for hh in heads:
            for d in (0, 1):
                sfin_ref[0, d, hh] = st_ref[2 * hh + d].T


def _hgrn_consts():
    c = HG_CHUNK
    t = np.arange(c)[:, None]
    s = np.arange(c)[None, :]
    tril = (s <= t)
    tri = jnp.asarray(np.stack([tril, tril.T]), BF16)
    fwd = [(t // HG_LEVELS[0] == s // HG_LEVELS[0]) & (s <= t)]
    for blk in HG_LEVELS[1:]:
        fwd.append((t // blk == s // blk) & (t % blk >= blk // 2) & (s % blk < blk // 2))
    fwd = np.stack(fwd)
    masks = jnp.asarray(np.stack([fwd, np.swapaxes(fwd, 1, 2)]), F32)
    return tri, masks


def _hgrn_mixer(pa, pb, lower, norm_w, tri, masks, layer, *, n_seq, seq_len, row_block0, s0):
    zero_init = s0 is None
    wide = HG_HEADS * DK
    seq_spec = lambda cb: pl.BlockSpec((seq_len, wide), lambda b, h: (row_block0 + b, cb // HG_HEADS + h))
    in_specs = [
        seq_spec(CB_HQ), seq_spec(CB_HI), seq_spec(CB_HG), seq_spec(CB_HFF), seq_spec(CB_HFB),
        pl.BlockSpec((1, 2, HG_HEADS, 1, DK), lambda b, h: (layer, 0, h, 0, 0)),
        pl.BlockSpec((1, 1, DK), lambda b, h: (layer, 0, 0)),
        pl.BlockSpec((2, HG_CHUNK, HG_CHUNK), lambda b, h: (0, 0, 0)),
        pl.BlockSpec((2, len(HG_LEVELS), HG_CHUNK, HG_CHUNK), lambda b, h: (0, 0, 0, 0)),
    ]
    args = [pa, pa, pa, pb, pb, lower, norm_w, tri, masks]
    if not zero_init:
        in_specs.append(pl.BlockSpec((1, 1, 2, HG_HEADS, DK, DK), lambda b, h: (b, layer, 0, h, 0, 0)))
        args.append(s0)
    out_specs = [pl.BlockSpec((seq_len, wide), lambda b, h: (b, h))]
    out_shape = [jax.ShapeDtypeStruct((n_seq * seq_len, BRANCH_W), BF16)]
    if zero_init:
        out_specs.append(pl.BlockSpec((1, 2, HG_HEADS, DK, DK), lambda b, h: (b, 0, h, 0, 0)))
        out_shape.append(jax.ShapeDtypeStruct((n_seq, 2, H_C, DK, DK), F32))
    res = pl.pallas_call(
        functools.partial(_hgrn_kernel, seq_len=seq_len, zero_init=zero_init, emit_state=zero_init),
        grid=(n_seq, H_C // HG_HEADS),
        in_specs=in_specs,
        out_specs=out_specs,
        out_shape=out_shape,
        scratch_shapes=[pltpu.VMEM((seq_len, wide), F32), pltpu.VMEM((seq_len, wide), F32),
                        pltpu.VMEM((2 * HG_HEADS, DK, DK), F32)],
        compiler_params=_params(("arbitrary", "arbitrary")),
        name="hgrn2_ctx" if zero_init else "hgrn2_latent",
    )(*args)
    return res if zero_init else (res[0], None)


def _merge_kernel(ya_ref, ybc_ref, ybl_ref, ycc_ref, ycl_ref, ga_ref, gb_ref, gc_ref, w_ref, o_ref, *, n_ctx_tiles):
    is_ctx = pl.program_id(1) < n_ctx_tiles
    yb = jnp.where(is_ctx, ybc_ref[...], ybl_ref[...])
    yc = jnp.where(is_ctx, ycc_ref[...], ycl_ref[...])
    acc = _sigmoid(ga_ref[...].astype(F32)) * _dot(ya_ref[...], w_ref[0, 0])
    acc = acc + _sigmoid(gb_ref[...].astype(F32)) * _dot(yb, w_ref[0, 1])
    acc = acc + _sigmoid(gc_ref[...].astype(F32)) * _dot(yc, w_ref[0, 2])
    o_ref[...] = acc.astype(BF16)


def _merge_branches(ya, yb_ctx, yb_lat, yc_ctx, yc_lat, pa, wb_bf, layer):
    tm, tn = 512, 1024
    nj = D_MODEL // tn
    nc = T_CTX // tm
    gate0 = CB_GATES * LANE // tn
    y_spec = pl.BlockSpec((tm, BRANCH_W), lambda j, i: (i, 0))
    ctx_spec = pl.BlockSpec((tm, BRANCH_W), lambda j, i: (jnp.minimum(i, nc - 1), 0))
    lat_spec = pl.BlockSpec((tm, BRANCH_W), lambda j, i: (jnp.maximum(i - nc, 0), 0))
    gate_spec = lambda k: pl.BlockSpec((tm, tn), lambda j, i: (i, gate0 + k * nj + j))
    return pl.pallas_call(
        functools.partial(_merge_kernel, n_ctx_tiles=nc),
        grid=(nj, T_ALL // tm),
        in_specs=[y_spec, ctx_spec, lat_spec, ctx_spec, lat_spec, gate_spec(0), gate_spec(1), gate_spec(2),
                  pl.BlockSpec((1, 3, BRANCH_W, tn), lambda j, i: (layer, 0, 0, j))],
        out_specs=pl.BlockSpec((tm, tn), lambda j, i: (i, j)),
        out_shape=jax.ShapeDtypeStruct((T_ALL, D_MODEL), BF16),
        compiler_params=_params(("arbitrary", "arbitrary")),
        name="merge_branches",
    )(ya, yb_ctx, yb_lat, yc_ctx, yc_lat, pa, pa, pa, wb_bf)


def _out_router_kernel(mg_ref, x_ref, mod_ref, wo_ref, lnw_ref, lnb_ref, rwh_ref, rwl_ref, rb_ref,
                       x1_ref, h2_ref, rt_ref, *, alpha):
    mix = _dot(mg_ref[...], wo_ref[0])
    g1 = mod_ref[0, 0, 2:3, :]
    x1 = _layer_norm_rows(alpha * x_ref[...] + g1 * mix, lnw_ref[0], lnb_ref[0])
    x1_ref[...] = x1
    h2 = x1 * (1.0 + mod_ref[0, 0, 4:5, :]) + mod_ref[0, 0, 3:4, :]
    h2_ref[...] = h2

    hh = h2.astype(BF16)
    hl = (h2 - hh.astype(F32)).astype(BF16)
    lg = _dot(hh, rwh_ref[0]) + _dot(hl, rwh_ref[0]) + _dot(hh, rwl_ref[0]) + rb_ref[0]

    lane = lax.broadcasted_iota(jnp.int32, lg.shape, 1)
    neg = jnp.float32(-jnp.inf)
    big = jnp.int32(1 << 20)
    is_c = lane < N_GROUPS
    cmax = jnp.max(jnp.where(is_c, lg, neg), axis=-1, keepdims=True)
    g_idx = jnp.min(jnp.where(is_c & (lg == cmax), lane, big), axis=-1, keepdims=True)
    p_group = 1.0 / jnp.sum(jnp.where(is_c, jnp.exp(lg - cmax), 0.0), axis=-1, keepdims=True)
    lo_lane = N_GROUPS + EXPERTS_PER_GROUP * g_idx
    in_g = (lane >= lo_lane) & (lane < lo_lane + EXPERTS_PER_GROUP)
    v1 = jnp.max(jnp.where(in_g, lg, neg), axis=-1, keepdims=True)
    i1 = jnp.min(jnp.where(in_g & (lg == v1), lane, big), axis=-1, keepdims=True)
    rest = in_g & (lane != i1)
    v2 = jnp.max(jnp.where(rest, lg, neg), axis=-1, keepdims=True)
    i2 = jnp.min(jnp.where(rest & (lg == v2), lane, big), axis=-1, keepdims=True)
    e21 = jnp.exp(v2 - v1)
    w1 = p_group / (1.0 + e21)
    w2 = p_group * e21 / (1.0 + e21)
    e1 = (i1 - N_GROUPS).astype(F32)
    e2 = (i2 - N_GROUPS).astype(F32)
    rt_ref[...] = jnp.where(lane == 0, e1, jnp.where(lane == 1, e2, jnp.where(lane == 2, w1, jnp.where(lane == 3, w2, 0.0))))


def _out_proj_router(merged, x, mods, wo_bf, ln_w, ln_b, rw_hi, rw_lo, rb, layer, alpha):
    tm = 256
    row = pl.BlockSpec((tm, D_MODEL), lambda i: (i, 0))
    vec = pl.BlockSpec((1, 1, D_MODEL), lambda i: (layer, 0, 0))
    rw_spec = pl.BlockSpec((1, D_MODEL, ROUTE_LANES), lambda i: (layer, 0, 0))
    return pl.pallas_call(
        functools.partial(_out_router_kernel, alpha=alpha),
        grid=(T_ALL // tm,),
        in_specs=[
            row, row, _mod_spec(layer, tm),
            pl.BlockSpec((1, D_MODEL, D_MODEL), lambda i: (layer, 0, 0)),
            vec, vec, rw_spec, rw_spec,
            pl.BlockSpec((1, 1, ROUTE_LANES), lambda i: (layer, 0, 0)),
        ],
        out_specs=[row, row, pl.BlockSpec((tm, ROUTE_LANES), lambda i: (i, 0))],
        out_shape=[jax.ShapeDtypeStruct((T_ALL, D_MODEL), F32), jax.ShapeDtypeStruct((T_ALL, D_MODEL), F32),
                   jax.ShapeDtypeStruct((T_ALL, ROUTE_LANES), F32)],
        compiler_params=_params(("arbitrary",)),
        name="out_proj_router",
    )(merged, x, mods, wo_bf, ln_w, ln_b, rw_hi, rw_lo, rb)


def _row_copy(src_hbm, src_row, dst_ref, dst_row, sem):
    return pltpu.make_async_copy(src_hbm.at[pl.ds(src_row, 1)], dst_ref.at[pl.ds(dst_row, 1)], sem)


def _row_gather(src_hbm, idx_of, dst_ref, sem, n_rows):
    def issue(r, carry):
        _row_copy(src_hbm, idx_of(r), dst_ref, r, sem).start()
        return carry
    lax.fori_loop(0, n_rows, issue, 0, unroll=ROW_ISSUE_UNROLL)


def _row_gather_unrolled(src_hbm, idx_of, dst_ref, sem, n_rows):
    for r in range(n_rows):
        _row_copy(src_hbm, idx_of(r), dst_ref, r, sem).start()


def _row_gather_wait(src_hbm, dst_ref, sem, n_rows):
    pltpu.make_async_copy(src_hbm.at[pl.ds(0, n_rows)], dst_ref, sem).wait()


def _moe_kernel(be_ref, nused_ref, tok_ref, tokn_ref, h2_hbm, w1_ref, w3_ref, w2_ref, o_ref, xbuf, sem):
    del be_ref
    i = pl.program_id(0)
    n_used = nused_ref[0]
    slot = lax.rem(i, 2)

    @pl.when((i == 0) & (n_used > 0))
    def _():
        _row_gather(h2_hbm, lambda r: tok_ref[0, 0, r], xbuf.at[0], sem.at[0], MOE_BLK)

    @pl.when(i < n_used)
    def _():
        _row_gather_wait(h2_hbm, xbuf.at[slot], sem.at[slot], MOE_BLK)
        xb = xbuf[slot].astype(BF16)
        rows_per = MOE_BLK // MOE_ISSUE_CHUNKS
        cols_per = D_EXPERT // MOE_ISSUE_CHUNKS
        hid = []
        for c in range(MOE_ISSUE_CHUNKS):
            for r in range(c * rows_per, (c + 1) * rows_per):
                _row_copy(h2_hbm, tokn_ref[0, 0, r], xbuf.at[1 - slot], r, sem.at[1 - slot]).start()
            cs = slice(c * cols_per, (c + 1) * cols_per)
            hid.append((_silu(_dot(xb, w1_ref[0, 0, :, cs])) * _dot(xb, w3_ref[0, 0, :, cs])).astype(BF16))
        o_ref[...] = _dot(jnp.concatenate(hid, axis=1), w2_ref[0, 0])

    @pl.when(i + 1 == n_used)
    def _():
        _row_gather_wait(h2_hbm, xbuf.at[1 - slot], sem.at[1 - slot], MOE_BLK)

    @pl.when(i >= n_used)
    def _():
        o_ref[...] = jnp.zeros(o_ref.shape, F32)


def _routed_experts(h2, block_expert, n_used, row_tok, w1_bf, w3_bf, w2_bf, layer):
    tok3 = row_tok.reshape(N_MOE_BLOCKS, 1, MOE_BLK)
    grid_spec = pltpu.PrefetchScalarGridSpec(
        num_scalar_prefetch=2,
        grid=(N_MOE_BLOCKS,),
        in_specs=[
            pl.BlockSpec((1, 1, MOE_BLK), lambda i, be, nu: (i, 0, 0), memory_space=pltpu.SMEM),
            pl.BlockSpec((1, 1, MOE_BLK), lambda i, be, nu: (jnp.minimum(i + 1, N_MOE_BLOCKS - 1), 0, 0),
                         memory_space=pltpu.SMEM),
            pl.BlockSpec(memory_space=pl.ANY),
            pl.BlockSpec((1, 1, D_MODEL, D_EXPERT), lambda i, be, nu: (layer, be[i], 0, 0)),
            pl.BlockSpec((1, 1, D_MODEL, D_EXPERT), lambda i, be, nu: (layer, be[i], 0, 0)),
            pl.BlockSpec((1, 1, D_EXPERT, D_MODEL), lambda i, be, nu: (layer, be[i], 0, 0)),
        ],
        out_specs=pl.BlockSpec((MOE_BLK, D_MODEL), lambda i, be, nu: (i, 0)),
        scratch_shapes=[pltpu.VMEM((2, MOE_BLK, D_MODEL), F32), pltpu.SemaphoreType.DMA((2,))],
    )
    return pl.pallas_call(
        _moe_kernel,
        grid_spec=grid_spec,
        out_shape=jax.ShapeDtypeStruct((N_MOE_ROWS, D_MODEL), F32),
        compiler_params=_params(("arbitrary",)),
        name="routed_experts",
    )(block_expert, n_used, tok3, tok3, h2, w1_bf, w3_bf, w2_bf)


def _combine_kernel(*refs, alpha, tm, last, n_ctx_tiles):
    if last:
        pos_ref, posn_ref, x1_ref, rt_ref, mod_ref, lnw_ref, lnb_ref, yb_hbm, yp_ref, ys_ref, buf, sem = refs
    else:
        pos_ref, posn_ref, x1_ref, rt_ref, mod_ref, modn_ref, lnw_ref, lnb_ref, yb_hbm, o_ref, h_ref, buf, sem = refs
    i = pl.program_id(0)
    slot = lax.rem(i, 2)

    @pl.when(i == 0)
    def _():
        for k in range(TOP_K):
            _row_gather(yb_hbm, lambda r, k=k: pos_ref[0, k, r], buf.at[k], sem.at[k], tm)

    for k in range(TOP_K):
        _row_gather_wait(yb_hbm, buf.at[TOP_K * slot + k], sem.at[TOP_K * slot + k], tm)
    for k in range(TOP_K):
        nxt = TOP_K * (1 - slot) + k
        _row_gather_unrolled(yb_hbm, lambda r, k=k: posn_ref[0, k, r], buf.at[nxt], sem.at[nxt], tm)
    g2 = mod_ref[0, 0, 5:6, :]
    rt = rt_ref[...]
    moe = rt[:, TOP_K:TOP_K + 1] * buf[TOP_K * slot]
    for k in range(1, TOP_K):
        moe = moe + rt[:, TOP_K + k:TOP_K + k + 1] * buf[TOP_K * slot + k]
    x2 = _layer_norm_rows(alpha * x1_ref[...] + g2 * moe, lnw_ref[0], lnb_ref[0])
    if last:
        @pl.when(i < n_ctx_tiles)
        def _():
            yp_ref[...] = x2

        @pl.when(i >= n_ctx_tiles)
        def _():
            ys_ref[...] = x2
    else:
        o_ref[...] = x2
        h_ref[...] = (x2 * (1.0 + modn_ref[0, 0, 1:2, :]) + modn_ref[0, 0, 0:1, :]).astype(BF16)

    @pl.when(i + 1 == pl.num_programs(0))
    def _():
        for k in range(TOP_K):
            nxt = TOP_K * (1 - slot) + k
            _row_gather_wait(yb_hbm, buf.at[nxt], sem.at[nxt], tm)


def _combine_norm(pos, x1, route, mods, ln_w, ln_b, yb, layer, alpha):
    tm = COMBINE_TILE
    n_tiles = T_ALL // tm
    nc = T_CTX // tm
    last = layer + 1 == DEPTH
    row = pl.BlockSpec((tm, D_MODEL), lambda i: (i, 0))
    vec = pl.BlockSpec((1, 1, D_MODEL), lambda i: (layer, 0, 0))
    in_specs = [
        pl.BlockSpec((1, TOP_K, tm), lambda i: (i, 0, 0), memory_space=pltpu.SMEM),
        pl.BlockSpec((1, TOP_K, tm), lambda i: (jnp.minimum(i + 1, n_tiles - 1), 0, 0), memory_space=pltpu.SMEM),
        row, pl.BlockSpec((tm, ROUTE_LANES), lambda i: (i, 0)), _mod_spec(layer, tm),
    ]
    args = [pos, pos, x1, route, mods]
    if not last:
        in_specs.append(_mod_spec(layer + 1, tm))
        args.append(mods)
    in_specs += [vec, vec, pl.BlockSpec(memory_space=pl.ANY)]
    args += [ln_w, ln_b, yb]
    if last:
        out_specs = [pl.BlockSpec((tm, D_MODEL), lambda i: (jnp.minimum(i, nc - 1), 0)),
                     pl.BlockSpec((tm, D_MODEL), lambda i: (jnp.maximum(i - nc, 0), 0))]
        out_shape = [jax.ShapeDtypeStruct((T_CTX, D_MODEL), F32), jax.ShapeDtypeStruct((T_LAT, D_MODEL), F32)]
    else:
        out_specs = [row, row]
        out_shape = [jax.ShapeDtypeStruct((T_ALL, D_MODEL), F32), jax.ShapeDtypeStruct((T_ALL, D_MODEL), BF16)]
    return pl.pallas_call(
        functools.partial(_combine_kernel, alpha=alpha, tm=tm, last=last, n_ctx_tiles=nc),
        grid=(n_tiles,),
        in_specs=in_specs,
        out_specs=out_specs,
        out_shape=out_shape,
        scratch_shapes=[pltpu.VMEM((2 * TOP_K, tm, D_MODEL), F32), pltpu.SemaphoreType.DMA((2 * TOP_K,))],
        compiler_params=_params(("arbitrary",)),
        name="combine_norm",
    )(*args)


def _dispatch_plan(route):
    flat_e = route[:, 0:TOP_K].astype(jnp.int32).reshape(-1)
    onehot = (flat_e[:, None] == jnp.arange(N_EXPERTS, dtype=jnp.int32)[None, :]).astype(jnp.int32)
    grp = 256
    oh3 = onehot.reshape(-1, grp, N_EXPERTS)
    tril = jnp.asarray(np.tril(np.ones((grp, grp), np.float32)))
    inner = jnp.einsum('ts,gse->gte', tril, oh3.astype(F32), precision=lax.Precision.HIGHEST).astype(jnp.int32)
    gsum = jnp.sum(oh3, axis=1)
    goff = jnp.cumsum(gsum, axis=0) - gsum
    csum = (inner + goff[:, None, :]).reshape(-1, N_EXPERTS)
    rank = jnp.sum(csum * onehot, axis=1) - 1
    counts = jnp.sum(gsum, axis=0)
    padded = (counts + MOE_BLK - 1) // MOE_BLK * MOE_BLK
    pad_end = jnp.cumsum(padded)
    pad_start = pad_end - padded
    dest = (jnp.sum(onehot * pad_start[None, :], axis=1) + rank).astype(jnp.int32)
    tok = jnp.arange(T_ALL * TOP_K, dtype=jnp.int32) // TOP_K
    row_tok = jnp.zeros((N_MOE_ROWS,), jnp.int32).at[dest].set(tok)
    blk_row0 = jnp.arange(N_MOE_BLOCKS, dtype=jnp.int32) * MOE_BLK
    block_expert = jnp.clip(jnp.sum((pad_end[None, :] <= blk_row0[:, None]).astype(jnp.int32), axis=1),
                            0, N_EXPERTS - 1).astype(jnp.int32)
    n_used = (pad_end[-1] // MOE_BLK).astype(jnp.int32).reshape(1)
    tm = COMBINE_TILE
    pos = dest.reshape(T_ALL // tm, tm, TOP_K).transpose(0, 2, 1)
    return block_expert, n_used, row_tok, pos


def _rope_tables():
    pos = np.arange(DEC_SEQ)
    row = (pos // GRID_W).astype(np.float32)
    col = (pos % GRID_W).astype(np.float32)
    n_freq = DH_B // 4
    inv = jnp.asarray(ROPE_BASE, F32) ** (-jnp.arange(n_freq, dtype=F32) / n_freq)
    ang = jnp.concatenate([jnp.asarray(row)[:, None] * inv, jnp.asarray(col)[:, None] * inv], axis=-1)
    cos, sin = jnp.cos(ang), jnp.sin(ang)
    return jnp.concatenate([cos, cos], axis=-1), jnp.concatenate([-sin, sin], axis=-1)


def _na_table_selectors():
    rows = DEC_SEQ // GRID_W
    n_groups = rows // NA_GROUP
    n_dy = 2 * MAX_KH - 1
    sel = np.zeros((3 * NA_GROUP, NA_UNION, n_dy), np.float32)
    for kind, g_rep in enumerate((0, 1, n_groups - 1)):
        u0 = min(max(g_rep * NA_GROUP - MAX_KH // 2, 0), rows - NA_UNION)
        for j in range(NA_GROUP):
            r = g_rep * NA_GROUP + j
            start = min(max(r - MAX_KH // 2, 0), rows - MAX_KH)
            for i in range(NA_UNION):
                if start <= u0 + i < start + MAX_KH:
                    sel[kind * NA_GROUP + j, i, u0 + i - r + MAX_KH - 1] = 1.0
    cols = np.arange(GRID_W)
    col_start = np.clip(cols - KW // 2, 0, GRID_W - KW)
    col_mask = (cols[None, :] >= col_start[:, None]) & (cols[None, :] < col_start[:, None] + KW)
    dx_idx = np.clip(cols[None, :] - cols[:, None] + KW - 1, 0, 2 * KW - 2)
    dx_onehot = (dx_idx[:, :, None] == np.arange(2 * KW - 1)[None, None, :]).astype(np.float32)
    valid = (sel.sum(-1) > 0)[:, None, None, :, None] & col_mask[None, None, :, None, :]
    return sel, dx_onehot, valid


def _na_bias_table(rpb):
    sel, dx_onehot, valid = _na_table_selectors()
    toep = jnp.einsum('qkx,hyx->hyqk', jnp.asarray(dx_onehot), rpb, precision=lax.Precision.HIGHEST)
    tab = jnp.einsum('viy,hyqk->vhqik', jnp.asarray(sel), toep, precision=lax.Precision.HIGHEST)
    tab = jnp.where(jnp.asarray(valid), tab, MASK_VALUE)
    return tab.reshape(3 * NA_GROUP, H_B, GRID_W, NA_UNION * GRID_W)


def kernel(x_prompt, x_sample, cache_k, cache_v, state_hgrn, c, c_ctx, w_mod, b_mod, w_in, gmlp_ws, gmlp_bs, na_rpb, hgrn_lb, hgrn_norm_w, w_branch, w_o, ln1_w, ln1_b, ln2_w, ln2_b, router_w1, router_b1, router_w2, router_b2, moe_w1, moe_w3, moe_w2):
    alpha = (2.0 * DEPTH) ** 0.25
    cond = jnp.concatenate([c_ctx[None, :], c, jnp.zeros((MOD_ROWS - 1 - DEC_BATCH, D_MODEL), F32)], axis=0)
    mods = _modulation(cond, w_mod, b_mod)

    p_lb = jax.nn.softmax(hgrn_lb.astype(F32), axis=0)
    lower = (jnp.cumsum(p_lb, axis=0) - p_lb[0:1]).reshape(DEPTH, 2, H_C, 1, DK)
    norm_w = hgrn_norm_w.reshape(DEPTH, 1, DK)
    tri, masks = _hgrn_consts()
    cos_t, sin_t = _rope_tables()

    w_in_bf = w_in.astype(BF16)
    ws_bf = gmlp_ws.astype(BF16)
    bs_b = jnp.repeat(jnp.swapaxes(gmlp_bs, 1, 2), LANE, axis=2)
    wb_bf = w_branch.astype(BF16)
    wo_bf = w_o.astype(BF16)
    rw = jnp.concatenate([router_w1, router_w2.transpose(0, 2, 1, 3).reshape(DEPTH, D_MODEL, N_EXPERTS),
                          jnp.zeros((DEPTH, D_MODEL, ROUTE_LANES - N_GROUPS - N_EXPERTS), F32)], axis=-1)
    rw_hi = rw.astype(BF16)
    rw_lo = (rw - rw_hi.astype(F32)).astype(BF16)
    rb = jnp.concatenate([router_b1, router_b2.reshape(DEPTH, N_EXPERTS),
                          jnp.zeros((DEPTH, ROUTE_LANES - N_GROUPS - N_EXPERTS), F32)], axis=-1).reshape(DEPTH, 1, ROUTE_LANES)
    w1_bf = moe_w1.astype(BF16)
    w3_bf = moe_w3.astype(BF16)
    w2_bf = moe_w2.astype(BF16)
    ln1w, ln1b = ln1_w.reshape(DEPTH, 1, D_MODEL), ln1_b.reshape(DEPTH, 1, D_MODEL)
    ln2w, ln2b = ln2_w.reshape(DEPTH, 1, D_MODEL), ln2_b.reshape(DEPTH, 1, D_MODEL)

    x, h = _stack_modulate(x_prompt, x_sample, mods)
    ks_out, vs_out, ss_out = [], [], []
    for l in range(DEPTH):
        pa, pb = _in_projection(h, w_in_bf, l)
        ya = _gmlp_mixer(pa, ws_bf, bs_b, l)
        yb_ctx = _context_attention(pa)
        yb_lat = _neighbourhood_attention(pa, cache_k, cache_v, cos_t, sin_t, _na_bias_table(na_rpb[l]), l)
        yc_ctx, s_ctx = _hgrn_mixer(pa, pb, lower, norm_w, tri, masks, l,
                                    n_seq=BATCH, seq_len=SEQ, row_block0=0, s0=None)
        yc_lat, _ = _hgrn_mixer(pa, pb, lower, norm_w, tri, masks, l,
                                n_seq=DEC_BATCH, seq_len=DEC_SEQ, row_block0=T_CTX // DEC_SEQ, s0=state_hgrn)
        merged = _merge_branches(ya, yb_ctx, yb_lat, yc_ctx, yc_lat, pa, wb_bf, l)
        x1, h2, route = _out_proj_router(merged, x, mods, wo_bf, ln1w, ln1b, rw_hi, rw_lo, rb, l, alpha)
        block_expert, n_used, row_tok, pos = _dispatch_plan(route)
        ybk = _routed_experts(h2, block_expert, n_used, row_tok, w1_bf, w3_bf, w2_bf, l)
        res = _combine_norm(pos, x1, route, mods, ln2w, ln2b, ybk, l, alpha)
        if l + 1 < DEPTH:
            x, h = res
        else:
            y_ctx, y_lat = res

        kv = pa[:T_CTX, CB_K * LANE:(CB_V + H_B) * LANE].astype(F32)
        ks_out.append(kv[:, :BRANCH_W].reshape(BATCH, SEQ, H_B, DH_B))
        vs_out.append(kv[:, BRANCH_W:].reshape(BATCH, SEQ, H_B, DH_B))
        ss_out.append(s_ctx)

    y_prompt = y_ctx.reshape(BATCH, SEQ, D_MODEL)
    y_sample = y_lat.reshape(DEC_BATCH, DEC_SEQ, D_MODEL)
    return (y_prompt, y_sample, jnp.stack(ks_out, axis=1), jnp.stack(vs_out, axis=1), jnp.stack(ss_out, axis=1))
```

```python
import functools

import numpy as np
import jax
import jax.numpy as jnp
from jax import lax
from jax.experimental import pallas as pl
from jax.experimental.pallas import tpu as pltpu

F32 = jnp.float32
BF16 = jnp.bfloat16

D_MODEL = 2048
BATCH = 16
SEQ = 256
DEPTH = 2
DEC_BATCH = 4
DEC_SEQ = 4096
PAST_LEN = 512
GRID_W = 64
BRANCH_W = D_MODEL // 2
CHUNK_A = 128
G_A = 8
DH_B = 128
H_B = 8
MAX_KH = 8
KW = 16
ROPE_BASE = 10000.0
DK = 128
H_C = 8
N_GROUPS = 4
EXPERTS_PER_GROUP = 8
N_EXPERTS = 32
TOP_K = 2
D_EXPERT = D_MODEL // 2
LN_EPS = 1e-5
MASK_VALUE = -1e30
F_MIN = 1e-6
IN_COLS = 16384

T_CTX = BATCH * SEQ
T_LAT = DEC_BATCH * DEC_SEQ
T_ALL = T_CTX + T_LAT
SLAB = 4096
MOD_ROWS = 8
LANE = 128

PROJ_TN = 2048
PB_TILE = 3
PA_COLS = IN_COLS - PROJ_TN
CB_Q, CB_K, CB_V = 16, 24, 32
CB_HQ, CB_HI, CB_HG = 40, 48, 56
CB_GATES = 64
CB_HFF, CB_HFB = 0, 8

HG_CHUNK = 128
HG_LEVELS = (8, 16, 32, 64, 128)
HG_FIN_TILE = 256
HG_MAX_HEADS = 4
HG_VMEM_BYTES_PER_ROW_LANE = 40
HG_VMEM_BUDGET = 44 * 1024 * 1024
NA_CTX_TILE = 512
NA_GROUP = 4
NA_UNION = NA_GROUP + MAX_KH
MOE_BLK = 256
N_MOE_ROWS = T_ALL * TOP_K + N_EXPERTS * MOE_BLK
N_MOE_BLOCKS = N_MOE_ROWS // MOE_BLK
ROUTE_LANES = 128
COMBINE_TILE = 256
ROW_ISSUE_UNROLL = 8

VMEM_LIMIT = 56 * 1024 * 1024


def _params(sem):
    return pltpu.CompilerParams(dimension_semantics=sem, vmem_limit_bytes=VMEM_LIMIT)


def _sigmoid(x):
    return 1.0 / (1.0 + jnp.exp(-x))


def _silu(x):
    return x * _sigmoid(x)


def _gelu_tanh(x):
    return x * (0.5 * (1.0 + jnp.tanh(0.7978845608028654 * (x + 0.044715 * (x * x * x)))))


def _layer_norm_rows(y, w, b):
    mu = jnp.mean(y, axis=-1, keepdims=True)
    yc = y - mu
    var = jnp.mean(yc * yc, axis=-1, keepdims=True)
    return yc * lax.rsqrt(var + LN_EPS) * w + b


def _dot(a, b):
    return jnp.dot(a, b, preferred_element_type=F32)


def _dot_nt(a, b):
    return lax.dot_general(a, b, (((1,), (1,)), ((), ())), preferred_element_type=F32)


def _dot_tn(a, b):
    return lax.dot_general(a, b, (((0,), (0,)), ((), ())), preferred_element_type=F32)


def _mod_kernel(c_ref, w_ref, b_ref, o_ref):
    o_ref[0] = _dot(_silu(c_ref[...]), w_ref[0]) + b_ref[0]


def _modulation(cond, w_mod, b_mod):
    tn = 1024
    n = 6 * D_MODEL
    out = pl.pallas_call(
        _mod_kernel,
        grid=(DEPTH, n // tn),
        in_specs=[
            pl.BlockSpec((MOD_ROWS, D_MODEL), lambda l, j: (0, 0)),
            pl.BlockSpec((1, D_MODEL, tn), lambda l, j: (l, 0, j)),
            pl.BlockSpec((1, 1, tn), lambda l, j: (l, 0, j)),
        ],
        out_specs=pl.BlockSpec((1, MOD_ROWS, tn), lambda l, j: (l, 0, j)),
        out_shape=jax.ShapeDtypeStruct((DEPTH, MOD_ROWS, n), F32),
        compiler_params=_params(("arbitrary", "arbitrary")),
        name="modulation",
    )(cond, w_mod, b_mod.reshape(DEPTH, 1, n))
    return out.reshape(DEPTH, MOD_ROWS, 6, D_MODEL)


def _mod_spec(layer, tm, grid_rank=1):
    if grid_rank == 1:
        return pl.BlockSpec((1, 1, 6, D_MODEL), lambda i: (layer, (i * tm) // SLAB, 0, 0))
    return pl.BlockSpec((1, 1, 6, D_MODEL), lambda j, i: (layer, (i * tm) // SLAB, 0, 0))


def _stack_kernel(xp_ref, xs_ref, mod_ref, x_ref, h_ref, *, n_ctx_tiles):
    x = jnp.where(pl.program_id(0) < n_ctx_tiles, xp_ref[...], xs_ref[...])
    x_ref[...] = x
    h_ref[...] = (x * (1.0 + mod_ref[0, 0, 1:2, :]) + mod_ref[0, 0, 0:1, :]).astype(BF16)


def _stack_modulate(x_prompt, x_sample, mods):
    tm = 512
    nc = T_CTX // tm
    row = pl.BlockSpec((tm, D_MODEL), lambda i: (i, 0))
    return pl.pallas_call(
        functools.partial(_stack_kernel, n_ctx_tiles=nc),
        grid=(T_ALL // tm,),
        in_specs=[
            pl.BlockSpec((tm, D_MODEL), lambda i: (jnp.minimum(i, nc - 1), 0)),
            pl.BlockSpec((tm, D_MODEL), lambda i: (jnp.maximum(i - nc, 0), 0)),
            _mod_spec(0, tm),
        ],
        out_specs=[row, row],
        out_shape=[jax.ShapeDtypeStruct((T_ALL, D_MODEL), F32), jax.ShapeDtypeStruct((T_ALL, D_MODEL), BF16)],
        compiler_params=_params(("arbitrary",)),
        name="stack_modulate",
    )(x_prompt.reshape(T_CTX, D_MODEL), x_sample.reshape(T_LAT, D_MODEL), mods)


def _inproj_kernel(h_ref, w_ref, o_ref):
    o_ref[...] = _dot(h_ref[...], w_ref[0]).astype(o_ref.dtype)


def _in_projection(h, w_in_bf, layer):
    tm, tn = 1024, PROJ_TN
    n_tiles = IN_COLS // tn

    def call(col_tile_of, n_out_tiles, dtype, name):
        return pl.pallas_call(
            _inproj_kernel,
            grid=(n_out_tiles, T_ALL // tm),
            in_specs=[
                pl.BlockSpec((tm, D_MODEL), lambda j, i: (i, 0)),
                pl.BlockSpec((1, D_MODEL, tn), lambda j, i: (layer, 0, col_tile_of(j))),
            ],
            out_specs=pl.BlockSpec((tm, tn), lambda j, i: (i, j)),
            out_shape=jax.ShapeDtypeStruct((T_ALL, n_out_tiles * tn), dtype),
            compiler_params=_params(("arbitrary", "arbitrary")),
            name=name,
        )(h, w_in_bf)

    pa = call(lambda j: j + (j >= PB_TILE).astype(jnp.int32), n_tiles - 1, BF16, "in_projection_a")
    pb = call(lambda j: j + PB_TILE, 1, F32, "in_projection_b")
    return pa, pb


def _gmlp_kernel(uv_ref, ws_ref, bs_ref, o_ref, *, tm):
    ge = _gelu_tanh(uv_ref[...].astype(F32))
    u = ge[:, :BRANCH_W]
    v = ge[:, BRANCH_W:]
    mu = jnp.mean(v, axis=-1, keepdims=True)
    vc = v - mu
    var = jnp.mean(vc * vc, axis=-1, keepdims=True)
    vn = (vc * lax.rsqrt(var + LN_EPS)).astype(BF16)
    for n in range(tm // CHUNK_A):
        r = slice(n * CHUNK_A, (n + 1) * CHUNK_A)
        for g in range(G_A):
            c = slice(g * LANE, (g + 1) * LANE)
            sv = _dot(ws_ref[0, g], vn[r, c]) + bs_ref[0, :, c]
            o_ref[r, c] = (u[r, c] * sv).astype(BF16)


def _gmlp_mixer(pa, ws_bf, bs_b, layer):
    tm = 256
    return pl.pallas_call(
        functools.partial(_gmlp_kernel, tm=tm),
        grid=(T_ALL // tm,),
        in_specs=[
            pl.BlockSpec((tm, 2 * BRANCH_W), lambda i: (i, 0)),
            pl.BlockSpec((1, G_A, CHUNK_A, CHUNK_A), lambda i: (layer, 0, 0, 0)),
            pl.BlockSpec((1, CHUNK_A, BRANCH_W), lambda i: (layer, 0, 0)),
        ],
        out_specs=pl.BlockSpec((tm, BRANCH_W), lambda i: (i, 0)),
        out_shape=jax.ShapeDtypeStruct((T_ALL, BRANCH_W), BF16),
        compiler_params=_params(("arbitrary",)),
        name="gmlp_mixer",
    )(pa, ws_bf, bs_b)


def _ctx_attn_kernel(q_ref, k_ref, v_ref, o_ref):
    scale = DH_B ** -0.5
    s = _dot_nt(q_ref[...], k_ref[...]) * scale
    m = jnp.max(s, axis=-1, keepdims=True)
    p = jnp.exp(s - m)
    den = jnp.sum(p, axis=-1, keepdims=True)
    o_ref[...] = (_dot(p.astype(BF16), v_ref[...]) / den).astype(BF16)


def _context_attention(pa):
    return pl.pallas_call(
        _ctx_attn_kernel,
        grid=(BATCH, H_B),
        in_specs=[
            pl.BlockSpec((SEQ, DH_B), lambda b, h: (b, CB_Q + h)),
            pl.BlockSpec((SEQ, DH_B), lambda b, h: (b, CB_K + h)),
            pl.BlockSpec((SEQ, DH_B), lambda b, h: (b, CB_V + h)),
        ],
        out_specs=pl.BlockSpec((SEQ, DH_B), lambda b, h: (b, h)),
        out_shape=jax.ShapeDtypeStruct((T_CTX, BRANCH_W), BF16),
        compiler_params=_params(("arbitrary", "arbitrary")),
        name="context_attention",
    )(pa, pa, pa)


def _na_kernel(q_ref, k_ref, v_ref, kc_ref, vc_ref, cos_ref, sin_ref, bias_ref, o_ref,
               qr_ref, kr_ref, octx_ref, mctx_ref, lctx_ref):
    scale = DH_B ** -0.5
    rows = DEC_SEQ // GRID_W
    kh = MAX_KH
    cos = cos_ref[...]
    sin = sin_ref[...]
    q = q_ref[...].astype(F32)
    k = k_ref[...].astype(F32)
    qr_ref[...] = (q * cos + pltpu.roll(q, DH_B // 2, 1) * sin).astype(BF16)
    kr_ref[...] = (k * cos + pltpu.roll(k, DH_B // 2, 1) * sin).astype(BF16)
    kc = kc_ref[0, 0].astype(BF16)
    vc = vc_ref[0, 0].astype(BF16)

    def ctx_step(t, carry):
        trow = pl.ds(pl.multiple_of(t * NA_CTX_TILE, NA_CTX_TILE), NA_CTX_TILE)
        s = _dot_nt(q_ref[trow, :], kc) * scale
        m = jnp.max(s, axis=-1, keepdims=True)
        p = jnp.exp(s - m)
        octx_ref[trow, :] = _dot(p.astype(BF16), vc)
        mctx_ref[trow, :] = jnp.broadcast_to(m, (NA_CTX_TILE, LANE))
        lctx_ref[trow, :] = jnp.broadcast_to(jnp.sum(p, axis=-1, keepdims=True), (NA_CTX_TILE, LANE))
        return carry

    lax.fori_loop(0, DEC_SEQ // NA_CTX_TILE, ctx_step, 0, unroll=4)

    n_groups = rows // NA_GROUP
    nq = NA_GROUP * GRID_W
    nk = NA_UNION * GRID_W

    def group_step(g, carry):
        u0 = jnp.clip(g * NA_GROUP - kh // 2, 0, rows - NA_UNION)
        kind = jnp.where(g == 0, 0, jnp.where(g == n_groups - 1, 2, 1))
        qrow = pl.ds(pl.multiple_of(g * nq, nq), nq)
        krow = pl.ds(pl.multiple_of(u0 * GRID_W, GRID_W), nk)
        bias = jnp.concatenate([bias_ref[kind * NA_GROUP + j, 0] for j in range(NA_GROUP)], axis=0)
        s_loc = _dot_nt(qr_ref[qrow, :], kr_ref[krow, :]) * scale + bias
        m_ctx = mctx_ref[qrow, :][:, 0:1]
        m = jnp.maximum(jnp.max(s_loc, axis=-1, keepdims=True), m_ctx)
        p_loc = jnp.exp(s_loc - m)
        w_ctx = jnp.exp(m_ctx - m)
        den = jnp.sum(p_loc, axis=-1, keepdims=True) + lctx_ref[qrow, :][:, 0:1] * w_ctx
        o = _dot(p_loc.astype(BF16), v_ref[krow, :]) + octx_ref[qrow, :] * w_ctx
        o_ref[qrow, :] = (o / den).astype(BF16)
        return carry

    lax.fori_loop(0, n_groups, group_step, 0, unroll=4)


def _neighbourhood_attention(pa, cache_k, cache_v, cos_t, sin_t, bias_tab, layer):
    ck = cache_k.reshape(DEC_BATCH, DEPTH, PAST_LEN, H_B * DH_B)
    cv = cache_v.reshape(DEC_BATCH, DEPTH, PAST_LEN, H_B * DH_B)
    lat = T_CTX // DEC_SEQ
    return pl.pallas_call(
        _na_kernel,
        grid=(DEC_BATCH, H_B),
        in_specs=[
            pl.BlockSpec((DEC_SEQ, DH_B), lambda b, h: (lat + b, CB_Q + h)),
            pl.BlockSpec((DEC_SEQ, DH_B), lambda b, h: (lat + b, CB_K + h)),
            pl.BlockSpec((DEC_SEQ, DH_B), lambda b, h: (lat + b, CB_V + h)),
            pl.BlockSpec((1, 1, PAST_LEN, DH_B), lambda b, h: (b, layer, 0, h)),
            pl.BlockSpec((1, 1, PAST_LEN, DH_B), lambda b, h: (b, layer, 0, h)),
            pl.BlockSpec((DEC_SEQ, DH_B), lambda b, h: (0, 0)),
            pl.BlockSpec((DEC_SEQ, DH_B), lambda b, h: (0, 0)),
            pl.BlockSpec((3 * NA_GROUP, 1, GRID_W, NA_UNION * GRID_W), lambda b, h: (0, h, 0, 0)),
        ],
        out_specs=pl.BlockSpec((DEC_SEQ, DH_B), lambda b, h: (b, h)),
        out_shape=jax.ShapeDtypeStruct((T_LAT, BRANCH_W), BF16),
        scratch_shapes=[pltpu.VMEM((DEC_SEQ, DH_B), BF16) for _ in range(2)]
        + [pltpu.VMEM((DEC_SEQ, DH_B), F32) for _ in range(3)],
        compiler_params=_params(("arbitrary", "arbitrary")),
        name="neighbourhood_attention",
    )(pa, pa, pa, ck, cv, cos_t, sin_t, bias_tab)


def _ref_rows(a, blk, r):
    pieces = [jnp.broadcast_to(a[b0 + r:b0 + r + 1, :], (blk, a.shape[1])) for b0 in range(0, HG_CHUNK, blk)]
    return pieces[0] if len(pieces) == 1 else jnp.concatenate(pieces, axis=0)


def _hgrn_chunk(q, z, vb, lb, tri, masks, st, backward):
    c = HG_CHUNK
    f = jnp.maximum(lb + (1.0 - lb) * _sigmoid(z), F_MIN)
    gl = jnp.log(f)
    kk = 1.0 - f
    hi = gl.astype(BF16)
    lo = (gl - hi.astype(F32)).astype(BF16)
    a = _dot(tri, hi) + _dot(tri, lo)

    d0 = a - _ref_rows(a, HG_LEVELS[0], HG_LEVELS[0] // 2 if backward else HG_LEVELS[0] // 2 - 1)
    qw = (q * jnp.exp(d0)).astype(BF16)
    kw = (kk * jnp.exp(-d0)).astype(BF16)
    scores = _dot_nt(qw, kw) * masks[0]
    for li, blk in enumerate(HG_LEVELS[1:], start=1):
        dl = a - _ref_rows(a, blk, blk // 2 if backward else blk // 2 - 1)
        w = jnp.exp(-jnp.abs(dl))
        scores = scores + _dot_nt((q * w).astype(BF16), (kk * w).astype(BF16)) * masks[li]
    o = _dot(scores.astype(BF16), vb)
    o = o + _dot_nt((q * jnp.exp(a)).astype(BF16), st.astype(BF16))
    a_tot = a[0:1, :] if backward else a[c - 1:c, :]
    kd = (kk * jnp.exp(a_tot - a)).astype(BF16)
    st_new = st * jnp.exp(a_tot) + _dot_tn(vb, kd)
    return o, st_new


def _hgrn_kernel(*refs, seq_len, zero_init, emit_state, n_heads):
    q_ref, v_ref, g_ref, zf_ref, zb_ref, lb_ref, nw_ref, tri_ref, msk_ref = refs[:9]
    pos = 9
    s0_ref = None
    if not zero_init:
        s0_ref = refs[pos]
        pos += 1
    o_ref = refs[pos]
    pos += 1
    sfin_ref = None
    if emit_state:
        sfin_ref = refs[pos]
        pos += 1
    of_ref, ob_ref, st_ref = refs[pos:pos + 3]
    nchunk = seq_len // HG_CHUNK
    n_lev = len(HG_LEVELS)
    heads = range(n_heads)
    hcol = [slice(hh * DK, (hh + 1) * DK) for hh in heads]
    for hh in heads:
        for d in (0, 1):
            if zero_init:
                st_ref[2 * hh + d] = jnp.zeros((DK, DK), F32)
            else:
                st_ref[2 * hh + d] = s0_ref[0, 0, d, hh].T

    def body(ci, carry):
        rf = pl.ds(pl.multiple_of(ci * HG_CHUNK, HG_CHUNK), HG_CHUNK)
        rb = pl.ds(pl.multiple_of((nchunk - 1 - ci) * HG_CHUNK, HG_CHUNK), HG_CHUNK)
        for hh in heads:
            c = hcol[hh]
            o_f, st_f = _hgrn_chunk(q_ref[rf, c].astype(F32), zf_ref[rf, c], v_ref[rf, c], lb_ref[0, 0, hh],
                                    tri_ref[0], [msk_ref[0, li] for li in range(n_lev)], st_ref[2 * hh], False)
            o_b, st_b = _hgrn_chunk(q_ref[rb, c].astype(F32), zb_ref[rb, c], v_ref[rb, c], lb_ref[0, 1, hh],
                                    tri_ref[1], [msk_ref[1, li] for li in range(n_lev)], st_ref[2 * hh + 1], True)
            st_ref[2 * hh] = st_f
            st_ref[2 * hh + 1] = st_b
            of_ref[rf, c] = o_f
            ob_ref[rb, c] = o_b
        return carry

    lax.fori_loop(0, nchunk, body, 0, unroll=2)

    nw = nw_ref[0]

    def finish(t, carry):
        rows = pl.ds(pl.multiple_of(t * HG_FIN_TILE, HG_FIN_TILE), HG_FIN_TILE)
        for hh in heads:
            c = hcol[hh]
            o = of_ref[rows, c] + ob_ref[rows, c]
            o = o * lax.rsqrt(jnp.mean(o * o, axis=-1, keepdims=True) + LN_EPS) * nw
            o_ref[rows, c] = (o * _silu(g_ref[rows, c].astype(F32))).astype(BF16)
        return carry

    lax.fori_loop(0, seq_len // HG_FIN_TILE, finish, 0)
    if emit_state:
        for hh in heads:
            for d in (0, 1):
                sfin_ref[0, d, hh] = st_ref[2 * hh + d].T


def _hgrn_consts():
    c = HG_CHUNK
    t = np.arange(c)[:, None]
    s = np.arange(c)[None, :]
    tril = (s <= t)
    tri = jnp.asarray(np.stack([tril, tril.T]), BF16)
    fwd = [(t // HG_LEVELS[0] == s // HG_LEVELS[0]) & (s <= t)]
    for blk in HG_LEVELS[1:]:
        fwd.append((t // blk == s // blk) & (t % blk >= blk // 2) & (s % blk < blk // 2))
    fwd = np.stack(fwd)
    masks = jnp.asarray(np.stack([fwd, np.swapaxes(fwd, 1, 2)]), F32)
    return tri, masks


def _hgrn_mixer(pa, pb, lower, norm_w, tri, masks, layer, *, n_seq, seq_len, row_block0, s0):
    zero_init = s0 is None
    n_heads = HG_MAX_HEADS
    while n_heads > 1 and n_heads * seq_len * DK * HG_VMEM_BYTES_PER_ROW_LANE > HG_VMEM_BUDGET:
        n_heads //= 2
    wide = n_heads * DK
    seq_spec = lambda cb: pl.BlockSpec((seq_len, wide), lambda b, h: (row_block0 + b, cb // n_heads + h))
    in_specs = [
        seq_spec(CB_HQ), seq_spec(CB_HI), seq_spec(CB_HG), seq_spec(CB_HFF), seq_spec(CB_HFB),
        pl.BlockSpec((1, 2, n_heads, 1, DK), lambda b, h: (layer, 0, h, 0, 0)),
        pl.BlockSpec((1, 1, DK), lambda b, h: (layer, 0, 0)),
        pl.BlockSpec((2, HG_CHUNK, HG_CHUNK), lambda b, h: (0, 0, 0)),
        pl.BlockSpec((2, len(HG_LEVELS), HG_CHUNK, HG_CHUNK), lambda b, h: (0, 0, 0, 0)),
    ]
    args = [pa, pa, pa, pb, pb, lower, norm_w, tri, masks]
    if not zero_init:
        in_specs.append(pl.BlockSpec((1, 1, 2, n_heads, DK, DK), lambda b, h: (b, layer, 0, h, 0, 0)))
        args.append(s0)
    out_specs = [pl.BlockSpec((seq_len, wide), lambda b, h: (b, h))]
    out_shape = [jax.ShapeDtypeStruct((n_seq * seq_len, BRANCH_W), BF16)]
    if zero_init:
        out_specs.append(pl.BlockSpec((1, 2, n_heads, DK, DK), lambda b, h: (b, 0, h, 0, 0)))
        out_shape.append(jax.ShapeDtypeStruct((n_seq, 2, H_C, DK, DK), F32))
    res = pl.pallas_call(
        functools.partial(_hgrn_kernel, seq_len=seq_len, zero_init=zero_init, emit_state=zero_init,
                          n_heads=n_heads),
        grid=(n_seq, H_C // n_heads),
        in_specs=in_specs,
        out_specs=out_specs,
        out_shape=out_shape,
        scratch_shapes=[pltpu.VMEM((seq_len, wide), F32), pltpu.VMEM((seq_len, wide), F32),
                        pltpu.VMEM((2 * n_heads, DK, DK), F32)],
        compiler_params=_params(("arbitrary", "arbitrary")),
        name="hgrn2_ctx" if zero_init else "hgrn2_latent",
    )(*args)
    return res if zero_init else (res[0], None)


def _merge_kernel(ya_ref, ybc_ref, ybl_ref, ycc_ref, ycl_ref, ga_ref, gb_ref, gc_ref, w_ref, o_ref, *, n_ctx_tiles):
    is_ctx = pl.program_id(1) < n_ctx_tiles
    yb = jnp.where(is_ctx, ybc_ref[...], ybl_ref[...])
    yc = jnp.where(is_ctx, ycc_ref[...], ycl_ref[...])
    acc = _sigmoid(ga_ref[...].astype(F32)) * _dot(ya_ref[...], w_ref[0, 0])
    acc = acc + _sigmoid(gb_ref[...].astype(F32)) * _dot(yb, w_ref[0, 1])
    acc = acc + _sigmoid(gc_ref[...].astype(F32)) * _dot(yc, w_ref[0, 2])
    o_ref[...] = acc.astype(BF16)


def _merge_branches(ya, yb_ctx, yb_lat, yc_ctx, yc_lat, pa, wb_bf, layer):
    tm, tn = 512, 1024
    nj = D_MODEL // tn
    nc = T_CTX // tm
    gate0 = CB_GATES * LANE // tn
    y_spec = pl.BlockSpec((tm, BRANCH_W), lambda j, i: (i, 0))
    ctx_spec = pl.BlockSpec((tm, BRANCH_W), lambda j, i: (jnp.minimum(i, nc - 1), 0))
    lat_spec = pl.BlockSpec((tm, BRANCH_W), lambda j, i: (jnp.maximum(i - nc, 0), 0))
    gate_spec = lambda k: pl.BlockSpec((tm, tn), lambda j, i: (i, gate0 + k * nj + j))
    return pl.pallas_call(
        functools.partial(_merge_kernel, n_ctx_tiles=nc),
        grid=(nj, T_ALL // tm),
        in_specs=[y_spec, ctx_spec, lat_spec, ctx_spec, lat_spec, gate_spec(0), gate_spec(1), gate_spec(2),
                  pl.BlockSpec((1, 3, BRANCH_W, tn), lambda j, i: (layer, 0, 0, j))],
        out_specs=pl.BlockSpec((tm, tn), lambda j, i: (i, j)),
        out_shape=jax.ShapeDtypeStruct((T_ALL, D_MODEL), BF16),
        compiler_params=_params(("arbitrary", "arbitrary")),
        name="merge_branches",
    )(ya, yb_ctx, yb_lat, yc_ctx, yc_lat, pa, pa, pa, wb_bf)


def _out_router_kernel(mg_ref, x_ref, mod_ref, wo_ref, lnw_ref, lnb_ref, rwh_ref, rwl_ref, rb_ref,
                       x1_ref, h2_ref, rt_ref, *, alpha):
    mix = _dot(mg_ref[...], wo_ref[0])
    g1 = mod_ref[0, 0, 2:3, :]
    x1 = _layer_norm_rows(alpha * x_ref[...] + g1 * mix, lnw_ref[0], lnb_ref[0])
    x1_ref[...] = x1
    h2 = x1 * (1.0 + mod_ref[0, 0, 4:5, :]) + mod_ref[0, 0, 3:4, :]
    h2_ref[...] = h2

    hh = h2.astype(BF16)
    hl = (h2 - hh.astype(F32)).astype(BF16)
    lg = _dot(hh, rwh_ref[0]) + _dot(hl, rwh_ref[0]) + _dot(hh, rwl_ref[0]) + rb_ref[0]

    lane = lax.broadcasted_iota(jnp.int32, lg.shape, 1)
    neg = jnp.float32(-jnp.inf)
    big = jnp.int32(1 << 20)
    is_c = lane < N_GROUPS
    cmax = jnp.max(jnp.where(is_c, lg, neg), axis=-1, keepdims=True)
    g_idx = jnp.min(jnp.where(is_c & (lg == cmax), lane, big), axis=-1, keepdims=True)
    p_group = 1.0 / jnp.sum(jnp.where(is_c, jnp.exp(lg - cmax), 0.0), axis=-1, keepdims=True)
    lo_lane = N_GROUPS + EXPERTS_PER_GROUP * g_idx
    in_g = (lane >= lo_lane) & (lane < lo_lane + EXPERTS_PER_GROUP)
    v1 = jnp.max(jnp.where(in_g, lg, neg), axis=-1, keepdims=True)
    i1 = jnp.min(jnp.where(in_g & (lg == v1), lane, big), axis=-1, keepdims=True)
    rest = in_g & (lane != i1)
    v2 = jnp.max(jnp.where(rest, lg, neg), axis=-1, keepdims=True)
    i2 = jnp.min(jnp.where(rest & (lg == v2), lane, big), axis=-1, keepdims=True)
    e21 = jnp.exp(v2 - v1)
    w1 = p_group / (1.0 + e21)
    w2 = p_group * e21 / (1.0 + e21)
    e1 = (i1 - N_GROUPS).astype(F32)
    e2 = (i2 - N_GROUPS).astype(F32)
    rt_ref[...] = jnp.where(lane == 0, e1, jnp.where(lane == 1, e2, jnp.where(lane == 2, w1, jnp.where(lane == 3, w2, 0.0))))


def _out_proj_router(merged, x, mods, wo_bf, ln_w, ln_b, rw_hi, rw_lo, rb, layer, alpha):
    tm = 256
    row = pl.BlockSpec((tm, D_MODEL), lambda i: (i, 0))
    vec = pl.BlockSpec((1, 1, D_MODEL), lambda i: (layer, 0, 0))
    rw_spec = pl.BlockSpec((1, D_MODEL, ROUTE_LANES), lambda i: (layer, 0, 0))
    return pl.pallas_call(
        functools.partial(_out_router_kernel, alpha=alpha),
        grid=(T_ALL // tm,),
        in_specs=[
            row, row, _mod_spec(layer, tm),
            pl.BlockSpec((1, D_MODEL, D_MODEL), lambda i: (layer, 0, 0)),
            vec, vec, rw_spec, rw_spec,
            pl.BlockSpec((1, 1, ROUTE_LANES), lambda i: (layer, 0, 0)),
        ],
        out_specs=[row, row, pl.BlockSpec((tm, ROUTE_LANES), lambda i: (i, 0))],
        out_shape=[jax.ShapeDtypeStruct((T_ALL, D_MODEL), F32), jax.ShapeDtypeStruct((T_ALL, D_MODEL), F32),
                   jax.ShapeDtypeStruct((T_ALL, ROUTE_LANES), F32)],
        compiler_params=_params(("arbitrary",)),
        name="out_proj_router",
    )(merged, x, mods, wo_bf, ln_w, ln_b, rw_hi, rw_lo, rb)


def _row_copy(src_hbm, src_row, dst_ref, dst_row, sem):
    return pltpu.make_async_copy(src_hbm.at[pl.ds(src_row, 1)], dst_ref.at[pl.ds(dst_row, 1)], sem)


def _row_gather(src_hbm, idx_of, dst_ref, sem, n_rows):
    def issue(r, carry):
        _row_copy(src_hbm, idx_of(r), dst_ref, r, sem).start()
        return carry
    lax.fori_loop(0, n_rows, issue, 0, unroll=ROW_ISSUE_UNROLL)


def _row_gather_unrolled(src_hbm, idx_of, dst_ref, sem, n_rows, both_priorities=False):
    for r in range(n_rows):
        _row_copy(src_hbm, idx_of(r), dst_ref, r, sem).start(priority=r % 2 if both_priorities else 0)


def _row_gather_wait(src_hbm, dst_ref, sem, n_rows):
    pltpu.make_async_copy(src_hbm.at[pl.ds(0, n_rows)], dst_ref, sem).wait()


def _moe_kernel(be_ref, nused_ref, tok_ref, tokn_ref, h2_hbm, w1_ref, w3_ref, w2_ref, o_ref, xbuf, sem):
    del be_ref
    i = pl.program_id(0)
    n_used = nused_ref[0]
    slot = lax.rem(i, 2)

    @pl.when((i == 0) & (n_used > 0))
    def _():
        _row_gather(h2_hbm, lambda r: tok_ref[0, 0, r], xbuf.at[0], sem.at[0], MOE_BLK)

    @pl.when(i < n_used)
    def _():
        _row_gather_wait(h2_hbm, xbuf.at[slot], sem.at[slot], MOE_BLK)
        _row_gather_unrolled(h2_hbm, lambda r: tokn_ref[0, 0, r], xbuf.at[1 - slot], sem.at[1 - slot], MOE_BLK)
        xb = xbuf[slot].astype(BF16)
        hid = _silu(_dot(xb, w1_ref[0, 0])) * _dot(xb, w3_ref[0, 0])
        o_ref[...] = _dot(hid.astype(BF16), w2_ref[0, 0])

    @pl.when(i + 1 == n_used)
    def _():
        _row_gather_wait(h2_hbm, xbuf.at[1 - slot], sem.at[1 - slot], MOE_BLK)

    @pl.when(i >= n_used)
    def _():
        o_ref[...] = jnp.zeros(o_ref.shape, F32)


def _routed_experts(h2, block_expert, n_used, row_tok, w1_bf, w3_bf, w2_bf, layer):
    tok3 = row_tok.reshape(N_MOE_BLOCKS, 1, MOE_BLK)
    grid_spec = pltpu.PrefetchScalarGridSpec(
        num_scalar_prefetch=2,
        grid=(N_MOE_BLOCKS,),
        in_specs=[
            pl.BlockSpec((1, 1, MOE_BLK), lambda i, be, nu: (i, 0, 0), memory_space=pltpu.SMEM),
            pl.BlockSpec((1, 1, MOE_BLK), lambda i, be, nu: (jnp.minimum(i + 1, N_MOE_BLOCKS - 1), 0, 0),
                         memory_space=pltpu.SMEM),
            pl.BlockSpec(memory_space=pl.ANY),
            pl.BlockSpec((1, 1, D_MODEL, D_EXPERT), lambda i, be, nu: (layer, be[i], 0, 0)),
            pl.BlockSpec((1, 1, D_MODEL, D_EXPERT), lambda i, be, nu: (layer, be[i], 0, 0)),
            pl.BlockSpec((1, 1, D_EXPERT, D_MODEL), lambda i, be, nu: (layer, be[i], 0, 0)),
        ],
        out_specs=pl.BlockSpec((MOE_BLK, D_MODEL), lambda i, be, nu: (i, 0)),
        scratch_shapes=[pltpu.VMEM((2, MOE_BLK, D_MODEL), F32), pltpu.SemaphoreType.DMA((2,))],
    )
    return pl.pallas_call(
        _moe_kernel,
        grid_spec=grid_spec,
        out_shape=jax.ShapeDtypeStruct((N_MOE_ROWS, D_MODEL), F32),
        compiler_params=_params(("arbitrary",)),
        name="routed_experts",
    )(block_expert, n_used, tok3, tok3, h2, w1_bf, w3_bf, w2_bf)


def _combine_kernel(*refs, alpha, tm, last, n_ctx_tiles):
    if last:
        pos_ref, posn_ref, x1_ref, rt_ref, mod_ref, lnw_ref, lnb_ref, yb_hbm, yp_ref, ys_ref, buf, sem = refs
    else:
        pos_ref, posn_ref, x1_ref, rt_ref, mod_ref, modn_ref, lnw_ref, lnb_ref, yb_hbm, o_ref, h_ref, buf, sem = refs
    i = pl.program_id(0)
    slot = lax.rem(i, 2)

    @pl.when(i == 0)
    def _():
        for k in range(TOP_K):
            _row_gather(yb_hbm, lambda r, k=k: pos_ref[0, k, r], buf.at[k], sem.at[k], tm)

    for k in range(TOP_K):
        _row_gather_wait(yb_hbm, buf.at[TOP_K * slot + k], sem.at[TOP_K * slot + k], tm)
    for k in range(TOP_K):
        nxt = TOP_K * (1 - slot) + k
        _row_gather_unrolled(yb_hbm, lambda r, k=k: posn_ref[0, k, r], buf.at[nxt], sem.at[nxt], tm,
                             both_priorities=True)
    g2 = mod_ref[0, 0, 5:6, :]
    rt = rt_ref[...]
    moe = rt[:, TOP_K:TOP_K + 1] * buf[TOP_K * slot]
    for k in range(1, TOP_K):
        moe = moe + rt[:, TOP_K + k:TOP_K + k + 1] * buf[TOP_K * slot + k]
    x2 = _layer_norm_rows(alpha * x1_ref[...] + g2 * moe, lnw_ref[0], lnb_ref[0])
    if last:
        @pl.when(i < n_ctx_tiles)
        def _():
            yp_ref[...] = x2

        @pl.when(i >= n_ctx_tiles)
        def _():
            ys_ref[...] = x2
    else:
        o_ref[...] = x2
        h_ref[...] = (x2 * (1.0 + modn_ref[0, 0, 1:2, :]) + modn_ref[0, 0, 0:1, :]).astype(BF16)

    @pl.when(i + 1 == pl.num_programs(0))
    def _():
        for k in range(TOP_K):
            nxt = TOP_K * (1 - slot) + k
            _row_gather_wait(yb_hbm, buf.at[nxt], sem.at[nxt], tm)


def _combine_norm(pos, x1, route, mods, ln_w, ln_b, yb, layer, alpha):
    tm = COMBINE_TILE
    n_tiles = T_ALL // tm
    nc = T_CTX // tm
    last = layer + 1 == DEPTH
    row = pl.BlockSpec((tm, D_MODEL), lambda i: (i, 0))
    vec = pl.BlockSpec((1, 1, D_MODEL), lambda i: (layer, 0, 0))
    in_specs = [
        pl.BlockSpec((1, TOP_K, tm), lambda i: (i, 0, 0), memory_space=pltpu.SMEM),
        pl.BlockSpec((1, TOP_K, tm), lambda i: (jnp.minimum(i + 1, n_tiles - 1), 0, 0), memory_space=pltpu.SMEM),
        row, pl.BlockSpec((tm, ROUTE_LANES), lambda i: (i, 0)), _mod_spec(layer, tm),
    ]
    args = [pos, pos, x1, route, mods]
    if not last:
        in_specs.append(_mod_spec(layer + 1, tm))
        args.append(mods)
    in_specs += [vec, vec, pl.BlockSpec(memory_space=pl.ANY)]
    args += [ln_w, ln_b, yb]
    if last:
        out_specs = [pl.BlockSpec((tm, D_MODEL), lambda i: (jnp.minimum(i, nc - 1), 0)),
                     pl.BlockSpec((tm, D_MODEL), lambda i: (jnp.maximum(i - nc, 0), 0))]
        out_shape = [jax.ShapeDtypeStruct((T_CTX, D_MODEL), F32), jax.ShapeDtypeStruct((T_LAT, D_MODEL), F32)]
    else:
        out_specs = [row, row]
        out_shape = [jax.ShapeDtypeStruct((T_ALL, D_MODEL), F32), jax.ShapeDtypeStruct((T_ALL, D_MODEL), BF16)]
    return pl.pallas_call(
        functools.partial(_combine_kernel, alpha=alpha, tm=tm, last=last, n_ctx_tiles=nc),
        grid=(n_tiles,),
        in_specs=in_specs,
        out_specs=out_specs,
        out_shape=out_shape,
        scratch_shapes=[pltpu.VMEM((2 * TOP_K, tm, D_MODEL), F32), pltpu.SemaphoreType.DMA((2 * TOP_K,))],
        compiler_params=_params(("arbitrary",)),
        name="combine_norm",
    )(*args)


def _dispatch_plan(route):
    flat_e = route[:, 0:TOP_K].astype(jnp.int32).reshape(-1)
    onehot = (flat_e[:, None] == jnp.arange(N_EXPERTS, dtype=jnp.int32)[None, :]).astype(jnp.int32)
    grp = 256
    oh3 = onehot.reshape(-1, grp, N_EXPERTS)
    tril = jnp.asarray(np.tril(np.ones((grp, grp), np.float32)))
    inner = jnp.einsum('ts,gse->gte', tril, oh3.astype(F32), precision=lax.Precision.HIGHEST).astype(jnp.int32)
    gsum = jnp.sum(oh3, axis=1)
    goff = jnp.cumsum(gsum, axis=0) - gsum
    csum = (inner + goff[:, None, :]).reshape(-1, N_EXPERTS)
    rank = jnp.sum(csum * onehot, axis=1) - 1
    counts = jnp.sum(gsum, axis=0)
    padded = (counts + MOE_BLK - 1) // MOE_BLK * MOE_BLK
    pad_end = jnp.cumsum(padded)
    pad_start = pad_end - padded
    dest = (jnp.sum(onehot * pad_start[None, :], axis=1) + rank).astype(jnp.int32)
    tok = jnp.arange(T_ALL * TOP_K, dtype=jnp.int32) // TOP_K
    row_tok = jnp.zeros((N_MOE_ROWS,), jnp.int32).at[dest].set(tok)
    blk_row0 = jnp.arange(N_MOE_BLOCKS, dtype=jnp.int32) * MOE_BLK
    block_expert = jnp.clip(jnp.sum((pad_end[None, :] <= blk_row0[:, None]).astype(jnp.int32), axis=1),
                            0, N_EXPERTS - 1).astype(jnp.int32)
    n_used = (pad_end[-1] // MOE_BLK).astype(jnp.int32).reshape(1)
    tm = COMBINE_TILE
    pos = dest.reshape(T_ALL // tm, tm, TOP_K).transpose(0, 2, 1)
    return block_expert, n_used, row_tok, pos


def _rope_tables():
    pos = np.arange(DEC_SEQ)
    row = (pos // GRID_W).astype(np.float32)
    col = (pos % GRID_W).astype(np.float32)
    n_freq = DH_B // 4
    inv = jnp.asarray(ROPE_BASE, F32) ** (-jnp.arange(n_freq, dtype=F32) / n_freq)
    ang = jnp.concatenate([jnp.asarray(row)[:, None] * inv, jnp.asarray(col)[:, None] * inv], axis=-1)
    cos, sin = jnp.cos(ang), jnp.sin(ang)
    return jnp.concatenate([cos, cos], axis=-1), jnp.concatenate([-sin, sin], axis=-1)


def _na_table_selectors():
    rows = DEC_SEQ // GRID_W
    n_groups = rows // NA_GROUP
    n_dy = 2 * MAX_KH - 1
    sel = np.zeros((3 * NA_GROUP, NA_UNION, n_dy), np.float32)
    for kind, g_rep in enumerate((0, 1, n_groups - 1)):
        u0 = min(max(g_rep * NA_GROUP - MAX_KH // 2, 0), rows - NA_UNION)
        for j in range(NA_GROUP):
            r = g_rep * NA_GROUP + j
            start = min(max(r - MAX_KH // 2, 0), rows - MAX_KH)
            for i in range(NA_UNION):
                if start <= u0 + i < start + MAX_KH:
                    sel[kind * NA_GROUP + j, i, u0 + i - r + MAX_KH - 1] = 1.0
    cols = np.arange(GRID_W)
    col_start = np.clip(cols - KW // 2, 0, GRID_W - KW)
    col_mask = (cols[None, :] >= col_start[:, None]) & (cols[None, :] < col_start[:, None] + KW)
    dx_idx = np.clip(cols[None, :] - cols[:, None] + KW - 1, 0, 2 * KW - 2)
    dx_onehot = (dx_idx[:, :, None] == np.arange(2 * KW - 1)[None, None, :]).astype(np.float32)
    valid = (sel.sum(-1) > 0)[:, None, None, :, None] & col_mask[None, None, :, None, :]
    return sel, dx_onehot, valid


def _na_bias_table(rpb):
    sel, dx_onehot, valid = _na_table_selectors()
    toep = jnp.einsum('qkx,hyx->hyqk', jnp.asarray(dx_onehot), rpb, precision=lax.Precision.HIGHEST)
    tab = jnp.einsum('viy,hyqk->vhqik', jnp.asarray(sel), toep, precision=lax.Precision.HIGHEST)
    tab = jnp.where(jnp.asarray(valid), tab, MASK_VALUE)
    return tab.reshape(3 * NA_GROUP, H_B, GRID_W, NA_UNION * GRID_W)


def kernel(x_prompt, x_sample, cache_k, cache_v, state_hgrn, c, c_ctx, w_mod, b_mod, w_in, gmlp_ws, gmlp_bs, na_rpb, hgrn_lb, hgrn_norm_w, w_branch, w_o, ln1_w, ln1_b, ln2_w, ln2_b, router_w1, router_b1, router_w2, router_b2, moe_w1, moe_w3, moe_w2):
    alpha = (2.0 * DEPTH) ** 0.25
    cond = jnp.concatenate([c_ctx[None, :], c, jnp.zeros((MOD_ROWS - 1 - DEC_BATCH, D_MODEL), F32)], axis=0)
    mods = _modulation(cond, w_mod, b_mod)

    p_lb = jax.nn.softmax(hgrn_lb.astype(F32), axis=0)
    lower = (jnp.cumsum(p_lb, axis=0) - p_lb[0:1]).reshape(DEPTH, 2, H_C, 1, DK)
    norm_w = hgrn_norm_w.reshape(DEPTH, 1, DK)
    tri, masks = _hgrn_consts()
    cos_t, sin_t = _rope_tables()

    w_in_bf = w_in.astype(BF16)
    ws_bf = gmlp_ws.astype(BF16)
    bs_b = jnp.repeat(jnp.swapaxes(gmlp_bs, 1, 2), LANE, axis=2)
    wb_bf = w_branch.astype(BF16)
    wo_bf = w_o.astype(BF16)
    rw = jnp.concatenate([router_w1, router_w2.transpose(0, 2, 1, 3).reshape(DEPTH, D_MODEL, N_EXPERTS),
                          jnp.zeros((DEPTH, D_MODEL, ROUTE_LANES - N_GROUPS - N_EXPERTS), F32)], axis=-1)
    rw_hi = rw.astype(BF16)
    rw_lo = (rw - rw_hi.astype(F32)).astype(BF16)
    rb = jnp.concatenate([router_b1, router_b2.reshape(DEPTH, N_EXPERTS),
                          jnp.zeros((DEPTH, ROUTE_LANES - N_GROUPS - N_EXPERTS), F32)], axis=-1).reshape(DEPTH, 1, ROUTE_LANES)
    w1_bf = moe_w1.astype(BF16)
    w3_bf = moe_w3.astype(BF16)
    w2_bf = moe_w2.astype(BF16)
    ln1w, ln1b = ln1_w.reshape(DEPTH, 1, D_MODEL), ln1_b.reshape(DEPTH, 1, D_MODEL)
    ln2w, ln2b = ln2_w.reshape(DEPTH, 1, D_MODEL), ln2_b.reshape(DEPTH, 1, D_MODEL)

    x, h = _stack_modulate(x_prompt, x_sample, mods)
    ks_out, vs_out, ss_out = [], [], []
    for l in range(DEPTH):
        pa, pb = _in_projection(h, w_in_bf, l)
        ya = _gmlp_mixer(pa, ws_bf, bs_b, l)
        yb_ctx = _context_attention(pa)
        yb_lat = _neighbourhood_attention(pa, cache_k, cache_v, cos_t, sin_t, _na_bias_table(na_rpb[l]), l)
        yc_ctx, s_ctx = _hgrn_mixer(pa, pb, lower, norm_w, tri, masks, l,
                                    n_seq=BATCH, seq_len=SEQ, row_block0=0, s0=None)
        yc_lat, _ = _hgrn_mixer(pa, pb, lower, norm_w, tri, masks, l,
                                n_seq=DEC_BATCH, seq_len=DEC_SEQ, row_block0=T_CTX // DEC_SEQ, s0=state_hgrn)
        merged = _merge_branches(ya, yb_ctx, yb_lat, yc_ctx, yc_lat, pa, wb_bf, l)
        x1, h2, route = _out_proj_router(merged, x, mods, wo_bf, ln1w, ln1b, rw_hi, rw_lo, rb, l, alpha)
        block_expert, n_used, row_tok, pos = _dispatch_plan(route)
        ybk = _routed_experts(h2, block_expert, n_used, row_tok, w1_bf, w3_bf, w2_bf, l)
        res = _combine_norm(pos, x1, route, mods, ln2w, ln2b, ybk, l, alpha)
        if l + 1 < DEPTH:
            x, h = res
        else:
            y_ctx, y_lat = res

        kv = pa[:T_CTX, CB_K * LANE:(CB_V + H_B) * LANE].astype(F32)
        ks_out.append(kv[:, :BRANCH_W].reshape(BATCH, SEQ, H_B, DH_B))
        vs_out.append(kv[:, BRANCH_W:].reshape(BATCH, SEQ, H_B, DH_B))
        ss_out.append(s_ctx)

    y_prompt = y_ctx.reshape(BATCH, SEQ, D_MODEL)
    y_sample = y_lat.reshape(DEC_BATCH, DEC_SEQ, D_MODEL)
    return (y_prompt, y_sample, jnp.stack(ks_out, axis=1), jnp.stack(vs_out, axis=1), jnp.stack(ss_out, axis=1))
```

```python
import functools

import numpy as np
import jax
import jax.numpy as jnp
from jax import lax
from jax.experimental import pallas as pl
from jax.experimental.pallas import tpu as pltpu

F32 = jnp.float32
BF16 = jnp.bfloat16

D_MODEL = 2048
BATCH = 16
SEQ = 256
DEPTH = 2
DEC_BATCH = 4
DEC_SEQ = 4096
PAST_LEN = 512
GRID_W = 64
BRANCH_W = D_MODEL // 2
CHUNK_A = 128
G_A = 8
DH_B = 128
H_B = 8
MAX_KH = 8
KW = 16
ROPE_BASE = 10000.0
DK = 128
H_C = 8
N_GROUPS = 4
EXPERTS_PER_GROUP = 8
N_EXPERTS = 32
TOP_K = 2
D_EXPERT = D_MODEL // 2
LN_EPS = 1e-5
MASK_VALUE = -1e30
F_MIN = 1e-6
IN_COLS = 16384

T_CTX = BATCH * SEQ
T_LAT = DEC_BATCH * DEC_SEQ
T_ALL = T_CTX + T_LAT
SLAB = 4096
MOD_ROWS = 8
LANE = 128

PROJ_TN = 2048
PB_TILE = 3
PA_COLS = IN_COLS - PROJ_TN
CB_Q, CB_K, CB_V = 16, 24, 32
CB_HQ, CB_HI, CB_HG = 40, 48, 56
CB_GATES = 64
CB_HFF, CB_HFB = 0, 8

HG_CHUNK = 128
HG_LEVELS = (8, 16, 32, 64, 128)
HG_FIN_TILE = 256
HG_MAX_HEADS = 4
HG_VMEM_BYTES_PER_ROW_LANE = 40
HG_VMEM_BUDGET = 44 * 1024 * 1024
NA_CTX_TILE = 512
NA_GROUP = 4
NA_UNION = NA_GROUP + MAX_KH
MOE_BLK = 256
N_MOE_ROWS = T_ALL * TOP_K + N_EXPERTS * MOE_BLK
N_MOE_BLOCKS = N_MOE_ROWS // MOE_BLK
ROUTE_LANES = 128
COMBINE_TILE = 256
ROW_ISSUE_UNROLL = 8

VMEM_LIMIT = 56 * 1024 * 1024


def _params(sem):
    return pltpu.CompilerParams(dimension_semantics=sem, vmem_limit_bytes=VMEM_LIMIT)


def _sigmoid(x):
    return 1.0 / (1.0 + jnp.exp(-x))


def _silu(x):
    return x * _sigmoid(x)


def _gelu_tanh(x):
    return x * (0.5 * (1.0 + jnp.tanh(0.7978845608028654 * (x + 0.044715 * (x * x * x)))))


def _layer_norm_rows(y, w, b):
    mu = jnp.mean(y, axis=-1, keepdims=True)
    yc = y - mu
    var = jnp.mean(yc * yc, axis=-1, keepdims=True)
    return yc * lax.rsqrt(var + LN_EPS) * w + b


def _dot(a, b):
    return jnp.dot(a, b, preferred_element_type=F32)


def _dot_nt(a, b):
    return lax.dot_general(a, b, (((1,), (1,)), ((), ())), preferred_element_type=F32)


def _dot_tn(a, b):
    return lax.dot_general(a, b, (((0,), (0,)), ((), ())), preferred_element_type=F32)


def _mod_kernel(c_ref, w_ref, b_ref, o_ref):
    o_ref[0] = _dot(_silu(c_ref[...]), w_ref[0]) + b_ref[0]


def _modulation(cond, w_mod, b_mod):
    tn = 1024
    n = 6 * D_MODEL
    out = pl.pallas_call(
        _mod_kernel,
        grid=(DEPTH, n // tn),
        in_specs=[
            pl.BlockSpec((MOD_ROWS, D_MODEL), lambda l, j: (0, 0)),
            pl.BlockSpec((1, D_MODEL, tn), lambda l, j: (l, 0, j)),
            pl.BlockSpec((1, 1, tn), lambda l, j: (l, 0, j)),
        ],
        out_specs=pl.BlockSpec((1, MOD_ROWS, tn), lambda l, j: (l, 0, j)),
        out_shape=jax.ShapeDtypeStruct((DEPTH, MOD_ROWS, n), F32),
        compiler_params=_params(("arbitrary", "arbitrary")),
        name="modulation",
    )(cond, w_mod, b_mod.reshape(DEPTH, 1, n))
    return out.reshape(DEPTH, MOD_ROWS, 6, D_MODEL)


def _mod_spec(layer, tm, grid_rank=1):
    if grid_rank == 1:
        return pl.BlockSpec((1, 1, 6, D_MODEL), lambda i: (layer, (i * tm) // SLAB, 0, 0))
    return pl.BlockSpec((1, 1, 6, D_MODEL), lambda j, i: (layer, (i * tm) // SLAB, 0, 0))


def _stack_kernel(xp_ref, xs_ref, mod_ref, x_ref, h_ref, *, n_ctx_tiles):
    x = jnp.where(pl.program_id(0) < n_ctx_tiles, xp_ref[...], xs_ref[...])
    x_ref[...] = x
    h_ref[...] = (x * (1.0 + mod_ref[0, 0, 1:2, :]) + mod_ref[0, 0, 0:1, :]).astype(BF16)


def _stack_modulate(x_prompt, x_sample, mods):
    tm = 512
    nc = T_CTX // tm
    row = pl.BlockSpec((tm, D_MODEL), lambda i: (i, 0))
    return pl.pallas_call(
        functools.partial(_stack_kernel, n_ctx_tiles=nc),
        grid=(T_ALL // tm,),
        in_specs=[
            pl.BlockSpec((tm, D_MODEL), lambda i: (jnp.minimum(i, nc - 1), 0)),
            pl.BlockSpec((tm, D_MODEL), lambda i: (jnp.maximum(i - nc, 0), 0)),
            _mod_spec(0, tm),
        ],
        out_specs=[row, row],
        out_shape=[jax.ShapeDtypeStruct((T_ALL, D_MODEL), F32), jax.ShapeDtypeStruct((T_ALL, D_MODEL), BF16)],
        compiler_params=_params(("arbitrary",)),
        name="stack_modulate",
    )(x_prompt.reshape(T_CTX, D_MODEL), x_sample.reshape(T_LAT, D_MODEL), mods)


def _inproj_kernel(h_ref, w_ref, o_ref):
    o_ref[...] = _dot(h_ref[...], w_ref[0]).astype(o_ref.dtype)


def _in_projection(h, w_in_bf, layer):
    tm, tn = 1024, PROJ_TN
    n_tiles = IN_COLS // tn

    def call(col_tile_of, n_out_tiles, dtype, name):
        return pl.pallas_call(
            _inproj_kernel,
            grid=(n_out_tiles, T_ALL // tm),
            in_specs=[
                pl.BlockSpec((tm, D_MODEL), lambda j, i: (i, 0)),
                pl.BlockSpec((1, D_MODEL, tn), lambda j, i: (layer, 0, col_tile_of(j))),
            ],
            out_specs=pl.BlockSpec((tm, tn), lambda j, i: (i, j)),
            out_shape=jax.ShapeDtypeStruct((T_ALL, n_out_tiles * tn), dtype),
            compiler_params=_params(("arbitrary", "arbitrary")),
            name=name,
        )(h, w_in_bf)

    pa = call(lambda j: j + (j >= PB_TILE).astype(jnp.int32), n_tiles - 1, BF16, "in_projection_a")
    pb = call(lambda j: j + PB_TILE, 1, F32, "in_projection_b")
    return pa, pb


def _gmlp_kernel(uv_ref, ws_ref, bs_ref, o_ref, *, tm):
    ge = _gelu_tanh(uv_ref[...].astype(F32))
    u = ge[:, :BRANCH_W]
    v = ge[:, BRANCH_W:]
    mu = jnp.mean(v, axis=-1, keepdims=True)
    vc = v - mu
    var = jnp.mean(vc * vc, axis=-1, keepdims=True)
    vn = (vc * lax.rsqrt(var + LN_EPS)).astype(BF16)
    for n in range(tm // CHUNK_A):
        r = slice(n * CHUNK_A, (n + 1) * CHUNK_A)
        for g in range(G_A):
            c = slice(g * LANE, (g + 1) * LANE)
            sv = _dot(ws_ref[0, g], vn[r, c]) + bs_ref[0, :, c]
            o_ref[r, c] = (u[r, c] * sv).astype(BF16)


def _gmlp_mixer(pa, ws_bf, bs_b, layer):
    tm = 256
    return pl.pallas_call(
        functools.partial(_gmlp_kernel, tm=tm),
        grid=(T_ALL // tm,),
        in_specs=[
            pl.BlockSpec((tm, 2 * BRANCH_W), lambda i: (i, 0)),
            pl.BlockSpec((1, G_A, CHUNK_A, CHUNK_A), lambda i: (layer, 0, 0, 0)),
            pl.BlockSpec((1, CHUNK_A, BRANCH_W), lambda i: (layer, 0, 0)),
        ],
        out_specs=pl.BlockSpec((tm, BRANCH_W), lambda i: (i, 0)),
        out_shape=jax.ShapeDtypeStruct((T_ALL, BRANCH_W), BF16),
        compiler_params=_params(("arbitrary",)),
        name="gmlp_mixer",
    )(pa, ws_bf, bs_b)


def _ctx_attn_kernel(q_ref, k_ref, v_ref, o_ref):
    scale = DH_B ** -0.5
    s = _dot_nt(q_ref[...], k_ref[...]) * scale
    m = jnp.max(s, axis=-1, keepdims=True)
    p = jnp.exp(s - m)
    den = jnp.sum(p, axis=-1, keepdims=True)
    o_ref[...] = (_dot(p.astype(BF16), v_ref[...]) / den).astype(BF16)


def _context_attention(pa):
    return pl.pallas_call(
        _ctx_attn_kernel,
        grid=(BATCH, H_B),
        in_specs=[
            pl.BlockSpec((SEQ, DH_B), lambda b, h: (b, CB_Q + h)),
            pl.BlockSpec((SEQ, DH_B), lambda b, h: (b, CB_K + h)),
            pl.BlockSpec((SEQ, DH_B), lambda b, h: (b, CB_V + h)),
        ],
        out_specs=pl.BlockSpec((SEQ, DH_B), lambda b, h: (b, h)),
        out_shape=jax.ShapeDtypeStruct((T_CTX, BRANCH_W), BF16),
        compiler_params=_params(("arbitrary", "arbitrary")),
        name="context_attention",
    )(pa, pa, pa)


def _na_kernel(q_ref, k_ref, v_ref, kc_ref, vc_ref, cos_ref, sin_ref, bias_ref, o_ref,
               qr_ref, kr_ref, octx_ref, mctx_ref, lctx_ref):
    scale = DH_B ** -0.5
    rows = DEC_SEQ // GRID_W
    kh = MAX_KH
    cos = cos_ref[...]
    sin = sin_ref[...]
    q = q_ref[...].astype(F32)
    k = k_ref[...].astype(F32)
    qr_ref[...] = (q * cos + pltpu.roll(q, DH_B // 2, 1) * sin).astype(BF16)
    kr_ref[...] = (k * cos + pltpu.roll(k, DH_B // 2, 1) * sin).astype(BF16)
    kc = kc_ref[0, 0].astype(BF16)
    vc = vc_ref[0, 0].astype(BF16)

    def ctx_step(t, carry):
        trow = pl.ds(pl.multiple_of(t * NA_CTX_TILE, NA_CTX_TILE), NA_CTX_TILE)
        s = _dot_nt(q_ref[trow, :], kc) * scale
        m = jnp.max(s, axis=-1, keepdims=True)
        p = jnp.exp(s - m)
        octx_ref[trow, :] = _dot(p.astype(BF16), vc)
        mctx_ref[trow, :] = jnp.broadcast_to(m, (NA_CTX_TILE, LANE))
        lctx_ref[trow, :] = jnp.broadcast_to(jnp.sum(p, axis=-1, keepdims=True), (NA_CTX_TILE, LANE))
        return carry

    lax.fori_loop(0, DEC_SEQ // NA_CTX_TILE, ctx_step, 0, unroll=4)

    n_groups = rows // NA_GROUP
    nq = NA_GROUP * GRID_W
    nk = NA_UNION * GRID_W

    def group_step(g, carry):
        u0 = jnp.clip(g * NA_GROUP - kh // 2, 0, rows - NA_UNION)
        kind = jnp.where(g == 0, 0, jnp.where(g == n_groups - 1, 2, 1))
        qrow = pl.ds(pl.multiple_of(g * nq, nq), nq)
        krow = pl.ds(pl.multiple_of(u0 * GRID_W, GRID_W), nk)
        bias = jnp.concatenate([bias_ref[kind * NA_GROUP + j, 0] for j in range(NA_GROUP)], axis=0)
        s_loc = _dot_nt(qr_ref[qrow, :], kr_ref[krow, :]) * scale + bias
        m_ctx = mctx_ref[qrow, :][:, 0:1]
        m = jnp.maximum(jnp.max(s_loc, axis=-1, keepdims=True), m_ctx)
        p_loc = jnp.exp(s_loc - m)
        w_ctx = jnp.exp(m_ctx - m)
        den = jnp.sum(p_loc, axis=-1, keepdims=True) + lctx_ref[qrow, :][:, 0:1] * w_ctx
        o = _dot(p_loc.astype(BF16), v_ref[krow, :]) + octx_ref[qrow, :] * w_ctx
        o_ref[qrow, :] = (o / den).astype(BF16)
        return carry

    lax.fori_loop(0, n_groups, group_step, 0, unroll=4)


def _neighbourhood_attention(pa, cache_k, cache_v, cos_t, sin_t, bias_tab, layer):
    ck = cache_k.reshape(DEC_BATCH, DEPTH, PAST_LEN, H_B * DH_B)
    cv = cache_v.reshape(DEC_BATCH, DEPTH, PAST_LEN, H_B * DH_B)
    lat = T_CTX // DEC_SEQ
    return pl.pallas_call(
        _na_kernel,
        grid=(DEC_BATCH, H_B),
        in_specs=[
            pl.BlockSpec((DEC_SEQ, DH_B), lambda b, h: (lat + b, CB_Q + h)),
            pl.BlockSpec((DEC_SEQ, DH_B), lambda b, h: (lat + b, CB_K + h)),
            pl.BlockSpec((DEC_SEQ, DH_B), lambda b, h: (lat + b, CB_V + h)),
            pl.BlockSpec((1, 1, PAST_LEN, DH_B), lambda b, h: (b, layer, 0, h)),
            pl.BlockSpec((1, 1, PAST_LEN, DH_B), lambda b, h: (b, layer, 0, h)),
            pl.BlockSpec((DEC_SEQ, DH_B), lambda b, h: (0, 0)),
            pl.BlockSpec((DEC_SEQ, DH_B), lambda b, h: (0, 0)),
            pl.BlockSpec((3 * NA_GROUP, 1, GRID_W, NA_UNION * GRID_W), lambda b, h: (0, h, 0, 0)),
        ],
        out_specs=pl.BlockSpec((DEC_SEQ, DH_B), lambda b, h: (b, h)),
        out_shape=jax.ShapeDtypeStruct((T_LAT, BRANCH_W), BF16),
        scratch_shapes=[pltpu.VMEM((DEC_SEQ, DH_B), BF16) for _ in range(2)]
        + [pltpu.VMEM((DEC_SEQ, DH_B), F32) for _ in range(3)],
        compiler_params=_params(("arbitrary", "arbitrary")),
        name="neighbourhood_attention",
    )(pa, pa, pa, ck, cv, cos_t, sin_t, bias_tab)


def _ref_rows(a, blk, r):
    pieces = [jnp.broadcast_to(a[b0 + r:b0 + r + 1, :], (blk, a.shape[1])) for b0 in range(0, HG_CHUNK, blk)]
    return pieces[0] if len(pieces) == 1 else jnp.concatenate(pieces, axis=0)


def _hgrn_chunk(q, z, vb, lb, tri, masks, st, backward):
    c = HG_CHUNK
    f = jnp.maximum(lb + (1.0 - lb) * _sigmoid(z), F_MIN)
    gl = jnp.log(f)
    kk = 1.0 - f
    hi = gl.astype(BF16)
    lo = (gl - hi.astype(F32)).astype(BF16)
    a = _dot(tri, hi) + _dot(tri, lo)

    d0 = a - _ref_rows(a, HG_LEVELS[0], HG_LEVELS[0] // 2 if backward else HG_LEVELS[0] // 2 - 1)
    qw = (q * jnp.exp(d0)).astype(BF16)
    kw = (kk * jnp.exp(-d0)).astype(BF16)
    scores = _dot_nt(qw, kw) * masks[0]
    for li, blk in enumerate(HG_LEVELS[1:], start=1):
        dl = a - _ref_rows(a, blk, blk // 2 if backward else blk // 2 - 1)
        w = jnp.exp(-jnp.abs(dl))
        scores = scores + _dot_nt((q * w).astype(BF16), (kk * w).astype(BF16)) * masks[li]
    o = _dot(scores.astype(BF16), vb)
    o = o + _dot_nt((q * jnp.exp(a)).astype(BF16), st.astype(BF16))
    a_tot = a[0:1, :] if backward else a[c - 1:c, :]
    kd = (kk * jnp.exp(a_tot - a)).astype(BF16)
    st_new = st * jnp.exp(a_tot) + _dot_tn(vb, kd)
    return o, st_new


def _hgrn_kernel(*refs, seq_len, zero_init, emit_state, n_heads):
    q_ref, v_ref, g_ref, zf_ref, zb_ref, lb_ref, nw_ref, tri_ref, msk_ref = refs[:9]
    pos = 9
    s0_ref = None
    if not zero_init:
        s0_ref = refs[pos]
        pos += 1
    o_ref = refs[pos]
    pos += 1
    sfin_ref = None
    if emit_state:
        sfin_ref = refs[pos]
        pos += 1
    of_ref, ob_ref, st_ref = refs[pos:pos + 3]
    nchunk = seq_len // HG_CHUNK
    n_lev = len(HG_LEVELS)
    heads = range(n_heads)
    hcol = [slice(hh * DK, (hh + 1) * DK) for hh in heads]
    for hh in heads:
        for d in (0, 1):
            if zero_init:
                st_ref[2 * hh + d] = jnp.zeros((DK, DK), F32)
            else:
                st_ref[2 * hh + d] = s0_ref[0, 0, d, hh].T

    def body(ci, carry):
        rf = pl.ds(pl.multiple_of(ci * HG_CHUNK, HG_CHUNK), HG_CHUNK)
        rb = pl.ds(pl.multiple_of((nchunk - 1 - ci) * HG_CHUNK, HG_CHUNK), HG_CHUNK)
        for hh in heads:
            c = hcol[hh]
            o_f, st_f = _hgrn_chunk(q_ref[rf, c].astype(F32), zf_ref[rf, c], v_ref[rf, c], lb_ref[0, 0, hh],
                                    tri_ref[0], [msk_ref[0, li] for li in range(n_lev)], st_ref[2 * hh], False)
            o_b, st_b = _hgrn_chunk(q_ref[rb, c].astype(F32), zb_ref[rb, c], v_ref[rb, c], lb_ref[0, 1, hh],
                                    tri_ref[1], [msk_ref[1, li] for li in range(n_lev)], st_ref[2 * hh + 1], True)
            st_ref[2 * hh] = st_f
            st_ref[2 * hh + 1] = st_b
            of_ref[rf, c] = o_f
            ob_ref[rb, c] = o_b
        return carry

    lax.fori_loop(0, nchunk, body, 0, unroll=2)

    nw = nw_ref[0]

    def finish(t, carry):
        rows = pl.ds(pl.multiple_of(t * HG_FIN_TILE, HG_FIN_TILE), HG_FIN_TILE)
        for hh in heads:
            c = hcol[hh]
            o = of_ref[rows, c] + ob_ref[rows, c]
            o = o * lax.rsqrt(jnp.mean(o * o, axis=-1, keepdims=True) + LN_EPS) * nw
            o_ref[rows, c] = (o * _silu(g_ref[rows, c].astype(F32))).astype(BF16)
        return carry

    lax.fori_loop(0, seq_len // HG_FIN_TILE, finish, 0)
    if emit_state:
        for hh in heads:
            for d in (0, 1):
                sfin_ref[0, d, hh] = st_ref[2 * hh + d].T


def _hgrn_consts():
    c = HG_CHUNK
    t = np.arange(c)[:, None]
    s = np.arange(c)[None, :]
    tril = (s <= t)
    tri = jnp.asarray(np.stack([tril, tril.T]), BF16)
    fwd = [(t // HG_LEVELS[0] == s // HG_LEVELS[0]) & (s <= t)]
    for blk in HG_LEVELS[1:]:
        fwd.append((t // blk == s // blk) & (t % blk >= blk // 2) & (s % blk < blk // 2))
    fwd = np.stack(fwd)
    masks = jnp.asarray(np.stack([fwd, np.swapaxes(fwd, 1, 2)]), F32)
    return tri, masks


def _hgrn_mixer(pa, pb, lower, norm_w, tri, masks, layer, *, n_seq, seq_len, row_block0, s0):
    zero_init = s0 is None
    n_heads = HG_MAX_HEADS
    while n_heads > 1 and n_heads * seq_len * DK * HG_VMEM_BYTES_PER_ROW_LANE > HG_VMEM_BUDGET:
        n_heads //= 2
    wide = n_heads * DK
    seq_spec = lambda cb: pl.BlockSpec((seq_len, wide), lambda b, h: (row_block0 + b, cb // n_heads + h))
    in_specs = [
        seq_spec(CB_HQ), seq_spec(CB_HI), seq_spec(CB_HG), seq_spec(CB_HFF), seq_spec(CB_HFB),
        pl.BlockSpec((1, 2, n_heads, 1, DK), lambda b, h: (layer, 0, h, 0, 0)),
        pl.BlockSpec((1, 1, DK), lambda b, h: (layer, 0, 0)),
        pl.BlockSpec((2, HG_CHUNK, HG_CHUNK), lambda b, h: (0, 0, 0)),
        pl.BlockSpec((2, len(HG_LEVELS), HG_CHUNK, HG_CHUNK), lambda b, h: (0, 0, 0, 0)),
    ]
    args = [pa, pa, pa, pb, pb, lower, norm_w, tri, masks]
    if not zero_init:
        in_specs.append(pl.BlockSpec((1, 1, 2, n_heads, DK, DK), lambda b, h: (b, layer, 0, h, 0, 0)))
        args.append(s0)
    out_specs = [pl.BlockSpec((seq_len, wide), lambda b, h: (b, h))]
    out_shape = [jax.ShapeDtypeStruct((n_seq * seq_len, BRANCH_W), BF16)]
    if zero_init:
        out_specs.append(pl.BlockSpec((1, 2, n_heads, DK, DK), lambda b, h: (b, 0, h, 0, 0)))
        out_shape.append(jax.ShapeDtypeStruct((n_seq, 2, H_C, DK, DK), F32))
    res = pl.pallas_call(
        functools.partial(_hgrn_kernel, seq_len=seq_len, zero_init=zero_init, emit_state=zero_init,
                          n_heads=n_heads),
        grid=(n_seq, H_C // n_heads),
        in_specs=in_specs,
        out_specs=out_specs,
        out_shape=out_shape,
        scratch_shapes=[pltpu.VMEM((seq_len, wide), F32), pltpu.VMEM((seq_len, wide), F32),
                        pltpu.VMEM((2 * n_heads, DK, DK), F32)],
        compiler_params=_params(("arbitrary", "arbitrary")),
        name="hgrn2_ctx" if zero_init else "hgrn2_latent",
    )(*args)
    return res if zero_init else (res[0], None)


def _merge_kernel(ya_ref, ybc_ref, ybl_ref, ycc_ref, ycl_ref, ga_ref, gb_ref, gc_ref, w_ref, o_ref, *, n_ctx_tiles):
    is_ctx = pl.program_id(1) < n_ctx_tiles
    yb = jnp.where(is_ctx, ybc_ref[...], ybl_ref[...])
    yc = jnp.where(is_ctx, ycc_ref[...], ycl_ref[...])
    acc = _sigmoid(ga_ref[...].astype(F32)) * _dot(ya_ref[...], w_ref[0, 0])
    acc = acc + _sigmoid(gb_ref[...].astype(F32)) * _dot(yb, w_ref[0, 1])
    acc = acc + _sigmoid(gc_ref[...].astype(F32)) * _dot(yc, w_ref[0, 2])
    o_ref[...] = acc.astype(BF16)


def _merge_branches(ya, yb_ctx, yb_lat, yc_ctx, yc_lat, pa, wb_bf, layer):
    tm, tn = 512, 1024
    nj = D_MODEL // tn
    nc = T_CTX // tm
    gate0 = CB_GATES * LANE // tn
    y_spec = pl.BlockSpec((tm, BRANCH_W), lambda j, i: (i, 0))
    ctx_spec = pl.BlockSpec((tm, BRANCH_W), lambda j, i: (jnp.minimum(i, nc - 1), 0))
    lat_spec = pl.BlockSpec((tm, BRANCH_W), lambda j, i: (jnp.maximum(i - nc, 0), 0))
    gate_spec = lambda k: pl.BlockSpec((tm, tn), lambda j, i: (i, gate0 + k * nj + j))
    return pl.pallas_call(
        functools.partial(_merge_kernel, n_ctx_tiles=nc),
        grid=(nj, T_ALL // tm),
        in_specs=[y_spec, ctx_spec, lat_spec, ctx_spec, lat_spec, gate_spec(0), gate_spec(1), gate_spec(2),
                  pl.BlockSpec((1, 3, BRANCH_W, tn), lambda j, i: (layer, 0, 0, j))],
        out_specs=pl.BlockSpec((tm, tn), lambda j, i: (i, j)),
        out_shape=jax.ShapeDtypeStruct((T_ALL, D_MODEL), BF16),
        compiler_params=_params(("arbitrary", "arbitrary")),
        name="merge_branches",
    )(ya, yb_ctx, yb_lat, yc_ctx, yc_lat, pa, pa, pa, wb_bf)


def _out_router_kernel(mg_ref, x_ref, mod_ref, wo_ref, lnw_ref, lnb_ref, rwh_ref, rwl_ref, rb_ref,
                       x1_ref, h2_ref, rt_ref, *, alpha):
    mix = _dot(mg_ref[...], wo_ref[0])
    g1 = mod_ref[0, 0, 2:3, :]
    x1 = _layer_norm_rows(alpha * x_ref[...] + g1 * mix, lnw_ref[0], lnb_ref[0])
    x1_ref[...] = x1
    h2 = x1 * (1.0 + mod_ref[0, 0, 4:5, :]) + mod_ref[0, 0, 3:4, :]
    h2_ref[...] = h2

    hh = h2.astype(BF16)
    hl = (h2 - hh.astype(F32)).astype(BF16)
    lg = _dot(hh, rwh_ref[0]) + _dot(hl, rwh_ref[0]) + _dot(hh, rwl_ref[0]) + rb_ref[0]

    lane = lax.broadcasted_iota(jnp.int32, lg.shape, 1)
    neg = jnp.float32(-jnp.inf)
    big = jnp.int32(1 << 20)
    is_c = lane < N_GROUPS
    cmax = jnp.max(jnp.where(is_c, lg, neg), axis=-1, keepdims=True)
    g_idx = jnp.min(jnp.where(is_c & (lg == cmax), lane, big), axis=-1, keepdims=True)
    p_group = 1.0 / jnp.sum(jnp.where(is_c, jnp.exp(lg - cmax), 0.0), axis=-1, keepdims=True)
    lo_lane = N_GROUPS + EXPERTS_PER_GROUP * g_idx
    in_g = (lane >= lo_lane) & (lane < lo_lane + EXPERTS_PER_GROUP)
    v1 = jnp.max(jnp.where(in_g, lg, neg), axis=-1, keepdims=True)
    i1 = jnp.min(jnp.where(in_g & (lg == v1), lane, big), axis=-1, keepdims=True)
    rest = in_g & (lane != i1)
    v2 = jnp.max(jnp.where(rest, lg, neg), axis=-1, keepdims=True)
    i2 = jnp.min(jnp.where(rest & (lg == v2), lane, big), axis=-1, keepdims=True)
    e21 = jnp.exp(v2 - v1)
    w1 = p_group / (1.0 + e21)
    w2 = p_group * e21 / (1.0 + e21)
    e1 = (i1 - N_GROUPS).astype(F32)
    e2 = (i2 - N_GROUPS).astype(F32)
    rt_ref[...] = jnp.where(lane == 0, e1, jnp.where(lane == 1, e2, jnp.where(lane == 2, w1, jnp.where(lane == 3, w2, 0.0))))


def _out_proj_router(merged, x, mods, wo_bf, ln_w, ln_b, rw_hi, rw_lo, rb, layer, alpha):
    tm = 256
    row = pl.BlockSpec((tm, D_MODEL), lambda i: (i, 0))
    vec = pl.BlockSpec((1, 1, D_MODEL), lambda i: (layer, 0, 0))
    rw_spec = pl.BlockSpec((1, D_MODEL, ROUTE_LANES), lambda i: (layer, 0, 0))
    return pl.pallas_call(
        functools.partial(_out_router_kernel, alpha=alpha),
        grid=(T_ALL // tm,),
        in_specs=[
            row, row, _mod_spec(layer, tm),
            pl.BlockSpec((1, D_MODEL, D_MODEL), lambda i: (layer, 0, 0)),
            vec, vec, rw_spec, rw_spec,
            pl.BlockSpec((1, 1, ROUTE_LANES), lambda i: (layer, 0, 0)),
        ],
        out_specs=[row, row, pl.BlockSpec((tm, ROUTE_LANES), lambda i: (i, 0))],
        out_shape=[jax.ShapeDtypeStruct((T_ALL, D_MODEL), F32), jax.ShapeDtypeStruct((T_ALL, D_MODEL), F32),
                   jax.ShapeDtypeStruct((T_ALL, ROUTE_LANES), F32)],
        compiler_params=_params(("arbitrary",)),
        name="out_proj_router",
    )(merged, x, mods, wo_bf, ln_w, ln_b, rw_hi, rw_lo, rb)


def _row_copy(src_hbm, src_row, dst_ref, dst_row, sem):
    return pltpu.make_async_copy(src_hbm.at[pl.ds(src_row, 1)], dst_ref.at[pl.ds(dst_row, 1)], sem)


def _row_gather(src_hbm, idx_of, dst_ref, sem, n_rows):
    def issue(r, carry):
        _row_copy(src_hbm, idx_of(r), dst_ref, r, sem).start()
        return carry
    lax.fori_loop(0, n_rows, issue, 0, unroll=ROW_ISSUE_UNROLL)


def _row_gather_unrolled(src_hbm, idx_of, dst_ref, sem, n_rows, both_priorities=False):
    for r in range(n_rows):
        _row_copy(src_hbm, idx_of(r), dst_ref, r, sem).start(priority=r % 2 if both_priorities else 0)


def _row_gather_wait(src_hbm, dst_ref, sem, n_rows):
    pltpu.make_async_copy(src_hbm.at[pl.ds(0, n_rows)], dst_ref, sem).wait()


def _moe_kernel(be_ref, nused_ref, tok_ref, tokn_ref, h2_hbm, w1_ref, w3_ref, w2_ref, o_ref, xbuf, sem):
    del be_ref
    i = pl.program_id(0)
    n_used = nused_ref[0]
    slot = lax.rem(i, 2)

    @pl.when((i == 0) & (n_used > 0))
    def _():
        _row_gather(h2_hbm, lambda r: tok_ref[0, 0, r], xbuf.at[0], sem.at[0], MOE_BLK)

    @pl.when(i < n_used)
    def _():
        _row_gather_wait(h2_hbm, xbuf.at[slot], sem.at[slot], MOE_BLK)
        _row_gather_unrolled(h2_hbm, lambda r: tokn_ref[0, 0, r], xbuf.at[1 - slot], sem.at[1 - slot], MOE_BLK)
        xb = xbuf[slot].astype(BF16)
        hid = _silu(_dot(xb, w1_ref[0, 0])) * _dot(xb, w3_ref[0, 0])
        o_ref[...] = _dot(hid.astype(BF16), w2_ref[0, 0])

    @pl.when(i + 1 == n_used)
    def _():
        _row_gather_wait(h2_hbm, xbuf.at[1 - slot], sem.at[1 - slot], MOE_BLK)

    @pl.when(i >= n_used)
    def _():
        o_ref[...] = jnp.zeros(o_ref.shape, F32)


def _routed_experts(h2, block_expert, n_used, row_tok, w1_bf, w3_bf, w2_bf, layer):
    tok3 = row_tok.reshape(N_MOE_BLOCKS, 1, MOE_BLK)
    grid_spec = pltpu.PrefetchScalarGridSpec(
        num_scalar_prefetch=2,
        grid=(N_MOE_BLOCKS,),
        in_specs=[
            pl.BlockSpec((1, 1, MOE_BLK), lambda i, be, nu: (i, 0, 0), memory_space=pltpu.SMEM),
            pl.BlockSpec((1, 1, MOE_BLK), lambda i, be, nu: (jnp.minimum(i + 1, N_MOE_BLOCKS - 1), 0, 0),
                         memory_space=pltpu.SMEM),
            pl.BlockSpec(memory_space=pl.ANY),
            pl.BlockSpec((1, 1, D_MODEL, D_EXPERT), lambda i, be, nu: (layer, be[i], 0, 0)),
            pl.BlockSpec((1, 1, D_MODEL, D_EXPERT), lambda i, be, nu: (layer, be[i], 0, 0)),
            pl.BlockSpec((1, 1, D_EXPERT, D_MODEL), lambda i, be, nu: (layer, be[i], 0, 0)),
        ],
        out_specs=pl.BlockSpec((MOE_BLK, D_MODEL), lambda i, be, nu: (i, 0)),
        scratch_shapes=[pltpu.VMEM((2, MOE_BLK, D_MODEL), F32), pltpu.SemaphoreType.DMA((2,))],
    )
    return pl.pallas_call(
        _moe_kernel,
        grid_spec=grid_spec,
        out_shape=jax.ShapeDtypeStruct((N_MOE_ROWS, D_MODEL), F32),
        compiler_params=_params(("arbitrary",)),
        name="routed_experts",
    )(block_expert, n_used, tok3, tok3, h2, w1_bf, w3_bf, w2_bf)


def _combine_kernel(*refs, alpha, tm, last, n_ctx_tiles):
    if last:
        pos_ref, posn_ref, x1_ref, rt_ref, mod_ref, lnw_ref, lnb_ref, yb_hbm, yp_ref, ys_ref, buf, sem = refs
    else:
        pos_ref, posn_ref, x1_ref, rt_ref, mod_ref, modn_ref, lnw_ref, lnb_ref, yb_hbm, o_ref, h_ref, buf, sem = refs
    i = pl.program_id(0)
    slot = lax.rem(i, 2)

    @pl.when(i == 0)
    def _():
        for k in range(TOP_K):
            _row_gather(yb_hbm, lambda r, k=k: pos_ref[0, k, r], buf.at[k], sem.at[k], tm)

    for k in range(TOP_K):
        _row_gather_wait(yb_hbm, buf.at[TOP_K * slot + k], sem.at[TOP_K * slot + k], tm)
    for k in range(TOP_K):
        nxt = TOP_K * (1 - slot) + k
        _row_gather_unrolled(yb_hbm, lambda r, k=k: posn_ref[0, k, r], buf.at[nxt], sem.at[nxt], tm,
                             both_priorities=True)
    g2 = mod_ref[0, 0, 5:6, :]
    rt = rt_ref[...]
    moe = rt[:, TOP_K:TOP_K + 1] * buf[TOP_K * slot]
    for k in range(1, TOP_K):
        moe = moe + rt[:, TOP_K + k:TOP_K + k + 1] * buf[TOP_K * slot + k]
    x2 = _layer_norm_rows(alpha * x1_ref[...] + g2 * moe, lnw_ref[0], lnb_ref[0])
    if last:
        @pl.when(i < n_ctx_tiles)
        def _():
            yp_ref[...] = x2

        @pl.when(i >= n_ctx_tiles)
        def _():
            ys_ref[...] = x2
    else:
        o_ref[...] = x2
        h_ref[...] = (x2 * (1.0 + modn_ref[0, 0, 1:2, :]) + modn_ref[0, 0, 0:1, :]).astype(BF16)

    @pl.when(i + 1 == pl.num_programs(0))
    def _():
        for k in range(TOP_K):
            nxt = TOP_K * (1 - slot) + k
            _row_gather_wait(yb_hbm, buf.at[nxt], sem.at[nxt], tm)


def _combine_norm(pos, x1, route, mods, ln_w, ln_b, yb, layer, alpha):
    tm = COMBINE_TILE
    n_tiles = T_ALL // tm
    nc = T_CTX // tm
    last = layer + 1 == DEPTH
    row = pl.BlockSpec((tm, D_MODEL), lambda i: (i, 0))
    vec = pl.BlockSpec((1, 1, D_MODEL), lambda i: (layer, 0, 0))
    in_specs = [
        pl.BlockSpec((1, TOP_K, tm), lambda i: (i, 0, 0), memory_space=pltpu.SMEM),
        pl.BlockSpec((1, TOP_K, tm), lambda i: (jnp.minimum(i + 1, n_tiles - 1), 0, 0), memory_space=pltpu.SMEM),
        row, pl.BlockSpec((tm, ROUTE_LANES), lambda i: (i, 0)), _mod_spec(layer, tm),
    ]
    args = [pos, pos, x1, route, mods]
    if not last:
        in_specs.append(_mod_spec(layer + 1, tm))
        args.append(mods)
    in_specs += [vec, vec, pl.BlockSpec(memory_space=pl.ANY)]
    args += [ln_w, ln_b, yb]
    if last:
        out_specs = [pl.BlockSpec((tm, D_MODEL), lambda i: (jnp.minimum(i, nc - 1), 0)),
                     pl.BlockSpec((tm, D_MODEL), lambda i: (jnp.maximum(i - nc, 0), 0))]
        out_shape = [jax.ShapeDtypeStruct((T_CTX, D_MODEL), F32), jax.ShapeDtypeStruct((T_LAT, D_MODEL), F32)]
    else:
        out_specs = [row, row]
        out_shape = [jax.ShapeDtypeStruct((T_ALL, D_MODEL), F32), jax.ShapeDtypeStruct((T_ALL, D_MODEL), BF16)]
    return pl.pallas_call(
        functools.partial(_combine_kernel, alpha=alpha, tm=tm, last=last, n_ctx_tiles=nc),
        grid=(n_tiles,),
        in_specs=in_specs,
        out_specs=out_specs,
        out_shape=out_shape,
        scratch_shapes=[pltpu.VMEM((2 * TOP_K, tm, D_MODEL), F32), pltpu.SemaphoreType.DMA((2 * TOP_K,))],
        compiler_params=_params(("arbitrary",)),
        name="combine_norm",
    )(*args)


def _dispatch_plan(route):
    flat_e = route[:, 0:TOP_K].astype(jnp.int32).reshape(-1)
    onehot = (flat_e[:, None] == jnp.arange(N_EXPERTS, dtype=jnp.int32)[None, :]).astype(jnp.int32)
    grp = 256
    oh3 = onehot.reshape(-1, grp, N_EXPERTS)
    tril = jnp.asarray(np.tril(np.ones((grp, grp), np.float32)))
    inner = jnp.einsum('ts,gse->gte', tril, oh3.astype(F32), precision=lax.Precision.HIGHEST).astype(jnp.int32)
    gsum = jnp.sum(oh3, axis=1)
    goff = jnp.cumsum(gsum, axis=0) - gsum
    csum = (inner + goff[:, None, :]).reshape(-1, N_EXPERTS)
    rank = jnp.sum(csum * onehot, axis=1) - 1
    counts = jnp.sum(gsum, axis=0)
    padded = (counts + MOE_BLK - 1) // MOE_BLK * MOE_BLK
    pad_end = jnp.cumsum(padded)
    pad_start = pad_end - padded
    dest = (jnp.sum(onehot * pad_start[None, :], axis=1) + rank).astype(jnp.int32)
    tok = jnp.arange(T_ALL * TOP_K, dtype=jnp.int32) // TOP_K
    row_tok = jnp.zeros((N_MOE_ROWS, 2), jnp.int32).at[dest].set(jnp.stack([tok, tok], axis=1))[:, 0]
    blk_row0 = jnp.arange(N_MOE_BLOCKS, dtype=jnp.int32) * MOE_BLK
    block_expert = jnp.clip(jnp.sum((pad_end[None, :] <= blk_row0[:, None]).astype(jnp.int32), axis=1),
                            0, N_EXPERTS - 1).astype(jnp.int32)
    n_used = (pad_end[-1] // MOE_BLK).astype(jnp.int32).reshape(1)
    tm = COMBINE_TILE
    pos = dest.reshape(T_ALL // tm, tm, TOP_K).transpose(0, 2, 1)
    return block_expert, n_used, row_tok, pos


def _rope_tables():
    pos = np.arange(DEC_SEQ)
    row = (pos // GRID_W).astype(np.float32)
    col = (pos % GRID_W).astype(np.float32)
    n_freq = DH_B // 4
    inv = jnp.asarray(ROPE_BASE, F32) ** (-jnp.arange(n_freq, dtype=F32) / n_freq)
    ang = jnp.concatenate([jnp.asarray(row)[:, None] * inv, jnp.asarray(col)[:, None] * inv], axis=-1)
    cos, sin = jnp.cos(ang), jnp.sin(ang)
    return jnp.concatenate([cos, cos], axis=-1), jnp.concatenate([-sin, sin], axis=-1)


def _na_table_selectors():
    rows = DEC_SEQ // GRID_W
    n_groups = rows // NA_GROUP
    n_dy = 2 * MAX_KH - 1
    sel = np.zeros((3 * NA_GROUP, NA_UNION, n_dy), np.float32)
    for kind, g_rep in enumerate((0, 1, n_groups - 1)):
        u0 = min(max(g_rep * NA_GROUP - MAX_KH // 2, 0), rows - NA_UNION)
        for j in range(NA_GROUP):
            r = g_rep * NA_GROUP + j
            start = min(max(r - MAX_KH // 2, 0), rows - MAX_KH)
            for i in range(NA_UNION):
                if start <= u0 + i < start + MAX_KH:
                    sel[kind * NA_GROUP + j, i, u0 + i - r + MAX_KH - 1] = 1.0
    cols = np.arange(GRID_W)
    col_start = np.clip(cols - KW // 2, 0, GRID_W - KW)
    col_mask = (cols[None, :] >= col_start[:, None]) & (cols[None, :] < col_start[:, None] + KW)
    dx_idx = np.clip(cols[None, :] - cols[:, None] + KW - 1, 0, 2 * KW - 2)
    dx_onehot = (dx_idx[:, :, None] == np.arange(2 * KW - 1)[None, None, :]).astype(np.float32)
    valid = (sel.sum(-1) > 0)[:, None, None, :, None] & col_mask[None, None, :, None, :]
    return sel, dx_onehot, valid


def _na_bias_table(rpb):
    sel, dx_onehot, valid = _na_table_selectors()
    toep = jnp.einsum('qkx,hyx->hyqk', jnp.asarray(dx_onehot), rpb, precision=lax.Precision.HIGHEST)
    tab = jnp.einsum('viy,hyqk->vhqik', jnp.asarray(sel), toep, precision=lax.Precision.HIGHEST)
    tab = jnp.where(jnp.asarray(valid), tab, MASK_VALUE)
    return tab.reshape(3 * NA_GROUP, H_B, GRID_W, NA_UNION * GRID_W)


def kernel(x_prompt, x_sample, cache_k, cache_v, state_hgrn, c, c_ctx, w_mod, b_mod, w_in, gmlp_ws, gmlp_bs, na_rpb, hgrn_lb, hgrn_norm_w, w_branch, w_o, ln1_w, ln1_b, ln2_w, ln2_b, router_w1, router_b1, router_w2, router_b2, moe_w1, moe_w3, moe_w2):
    alpha = (2.0 * DEPTH) ** 0.25
    cond = jnp.concatenate([c_ctx[None, :], c, jnp.zeros((MOD_ROWS - 1 - DEC_BATCH, D_MODEL), F32)], axis=0)
    mods = _modulation(cond, w_mod, b_mod)

    p_lb = jax.nn.softmax(hgrn_lb.astype(F32), axis=0)
    lower = (jnp.cumsum(p_lb, axis=0) - p_lb[0:1]).reshape(DEPTH, 2, H_C, 1, DK)
    norm_w = hgrn_norm_w.reshape(DEPTH, 1, DK)
    tri, masks = _hgrn_consts()
    cos_t, sin_t = _rope_tables()

    w_in_bf = w_in.astype(BF16)
    ws_bf = gmlp_ws.astype(BF16)
    bs_b = jnp.repeat(jnp.swapaxes(gmlp_bs, 1, 2), LANE, axis=2)
    wb_bf = w_branch.astype(BF16)
    wo_bf = w_o.astype(BF16)
    rw = jnp.concatenate([router_w1, router_w2.transpose(0, 2, 1, 3).reshape(DEPTH, D_MODEL, N_EXPERTS),
                          jnp.zeros((DEPTH, D_MODEL, ROUTE_LANES - N_GROUPS - N_EXPERTS), F32)], axis=-1)
    rw_hi = rw.astype(BF16)
    rw_lo = (rw - rw_hi.astype(F32)).astype(BF16)
    rb = jnp.concatenate([router_b1, router_b2.reshape(DEPTH, N_EXPERTS),
                          jnp.zeros((DEPTH, ROUTE_LANES - N_GROUPS - N_EXPERTS), F32)], axis=-1).reshape(DEPTH, 1, ROUTE_LANES)
    w1_bf = moe_w1.astype(BF16)
    w3_bf = moe_w3.astype(BF16)
    w2_bf = moe_w2.astype(BF16)
    ln1w, ln1b = ln1_w.reshape(DEPTH, 1, D_MODEL), ln1_b.reshape(DEPTH, 1, D_MODEL)
    ln2w, ln2b = ln2_w.reshape(DEPTH, 1, D_MODEL), ln2_b.reshape(DEPTH, 1, D_MODEL)

    x, h = _stack_modulate(x_prompt, x_sample, mods)
    ks_out, vs_out, ss_out = [], [], []
    for l in range(DEPTH):
        pa, pb = _in_projection(h, w_in_bf, l)
        ya = _gmlp_mixer(pa, ws_bf, bs_b, l)
        yb_ctx = _context_attention(pa)
        yb_lat = _neighbourhood_attention(pa, cache_k, cache_v, cos_t, sin_t, _na_bias_table(na_rpb[l]), l)
        yc_ctx, s_ctx = _hgrn_mixer(pa, pb, lower, norm_w, tri, masks, l,
                                    n_seq=BATCH, seq_len=SEQ, row_block0=0, s0=None)
        yc_lat, _ = _hgrn_mixer(pa, pb, lower, norm_w, tri, masks, l,
                                n_seq=DEC_BATCH, seq_len=DEC_SEQ, row_block0=T_CTX // DEC_SEQ, s0=state_hgrn)
        merged = _merge_branches(ya, yb_ctx, yb_lat, yc_ctx, yc_lat, pa, wb_bf, l)
        x1, h2, route = _out_proj_router(merged, x, mods, wo_bf, ln1w, ln1b, rw_hi, rw_lo, rb, l, alpha)
        block_expert, n_used, row_tok, pos = _dispatch_plan(route)
        ybk = _routed_experts(h2, block_expert, n_used, row_tok, w1_bf, w3_bf, w2_bf, l)
        res = _combine_norm(pos, x1, route, mods, ln2w, ln2b, ybk, l, alpha)
        if l + 1 < DEPTH:
            x, h = res
        else:
            y_ctx, y_lat = res

        kv = pa[:T_CTX, CB_K * LANE:(CB_V + H_B) * LANE].astype(F32)
        ks_out.append(kv[:, :BRANCH_W].reshape(BATCH, SEQ, H_B, DH_B))
        vs_out.append(kv[:, BRANCH_W:].reshape(BATCH, SEQ, H_B, DH_B))
        ss_out.append(s_ctx)

    y_prompt = y_ctx.reshape(BATCH, SEQ, D_MODEL)
    y_sample = y_lat.reshape(DEC_BATCH, DEC_SEQ, D_MODEL)
    return (y_prompt, y_sample, jnp.stack(ks_out, axis=1), jnp.stack(vs_out, axis=1), jnp.stack(ss_out, axis=1))
```
